```python
import jax, jax.numpy as jnp
from jax import lax
import numpy as np

D_MODEL = 2048
BATCH = 2
SEQ = 16384
DEPTH = 1

CTX_LEN = 256
GRID_W = 64
ROPE_BASE = 10000.0
EPS = 1e-6
NEG = -1e30
BLOCK = 128

MLA_HEADS = 8
MLA_NOPE = 128
MLA_ROPE = 64
MLA_V = 128
MLA_Q_RANK = 512
MLA_KV_RANK = 512
MLA_WIDTH = MLA_HEADS * MLA_V

SWA_HEADS = 16
SWA_KV_HEADS = 4
SWA_GROUP = SWA_HEADS // SWA_KV_HEADS
SWA_HEAD_DIM = 64
SWA_WIDTH = SWA_HEADS * SWA_HEAD_DIM
SWA_KV_WIDTH = SWA_KV_HEADS * SWA_HEAD_DIM
WINDOW = 128

IN_SPLITS = (MLA_Q_RANK, MLA_KV_RANK, MLA_ROPE, MLA_WIDTH,
             SWA_WIDTH, SWA_KV_WIDTH, SWA_KV_WIDTH, SWA_WIDTH,
             D_MODEL, D_MODEL)
IN_WIDTH = 512 + 512 + 64 + 1024 + 1024 + 256 + 256 + 1024 + 2 * D_MODEL

kernel_name = "hybrid_mla_swa_sink_gated_dit_block"


def _rmsnorm(x, g):
    xf = x.astype(jnp.float32)
    y = xf * lax.rsqrt(jnp.mean(xf * xf, axis=-1, keepdims=True) + EPS)
    return (y * g.astype(jnp.float32)).astype(x.dtype)


def _rope_half(x, pos):
    n = x.shape[-1] // 2
    inv = ROPE_BASE ** (-jnp.arange(n, dtype=jnp.float32) / n)
    ang = pos.astype(jnp.float32)[:, None] * inv[None, :]
    shape = (x.shape[1],) + (1,) * (x.ndim - 3) + (n,)
    cos = jnp.cos(ang).reshape(shape).astype(x.dtype)
    sin = jnp.sin(ang).reshape(shape).astype(x.dtype)
    x1, x2 = x[..., :n], x[..., n:]
    return jnp.concatenate([x1 * cos - x2 * sin, x2 * cos + x1 * sin], axis=-1)


def _rope_2d(x, rows, cols):
    a = x.shape[-1] // 2
    return jnp.concatenate([_rope_half(x[..., :a], rows), _rope_half(x[..., a:], cols)], axis=-1)


def _branch_inputs(h, w_in, b_in, qn_g, kvn_g, w_uq, w_ukv):
    B, S, _ = h.shape
    offsets, acc = [], 0
    for w in IN_SPLITS[:-1]:
        acc += w
        offsets.append(acc)
    z = h @ w_in + b_in
    c_q, c_kv, k_pe, gate_a, q_s, k_s, v_s, gate_b, mg_a, mg_b = jnp.split(z, offsets, axis=-1)
    q_mla = (_rmsnorm(c_q, qn_g) @ w_uq).reshape(B, S, MLA_HEADS, MLA_NOPE + MLA_ROPE)
    kv = (_rmsnorm(c_kv, kvn_g) @ w_ukv).reshape(B, S, MLA_HEADS, MLA_NOPE + MLA_V)
    k_nope, v_mla = kv[..., :MLA_NOPE], kv[..., MLA_NOPE:]
    q_s = q_s.reshape(B, S, SWA_KV_HEADS, SWA_GROUP, SWA_HEAD_DIM)
    k_s = k_s.reshape(B, S, SWA_KV_HEADS, SWA_HEAD_DIM)
    v_s = v_s.reshape(B, S, SWA_KV_HEADS, SWA_HEAD_DIM)
    return q_mla, k_nope, v_mla, k_pe, gate_a, q_s, k_s, v_s, gate_b, mg_a, mg_b


def _mla_keys(k_nope, k_pe):
    B, S, H, _ = k_nope.shape
    return jnp.concatenate([k_nope, jnp.broadcast_to(k_pe[:, :, None, :], (B, S, H, MLA_ROPE))], axis=-1)


def _merge(a_heads, gate_a, b_heads, gate_b, mg_a, mg_b, w_ba, w_bb, w_o):
    y_a = (a_heads * jax.nn.silu(gate_a)) @ w_ba
    y_b = (b_heads * jax.nn.silu(gate_b)) @ w_bb
    return (jax.nn.sigmoid(mg_a) * y_a + jax.nn.sigmoid(mg_b) * y_b) @ w_o


def _mla_latent(q, k, v, k_ctx, v_ctx):
    B, S, H, dq = q.shape
    scale = dq ** -0.5
    k_all = jnp.concatenate([k, k_ctx], axis=1)
    v_all = jnp.concatenate([v, v_ctx], axis=1)

    def blk(i):
        qb = lax.dynamic_slice_in_dim(q, i * BLOCK, BLOCK, axis=1)
        s = jnp.einsum('bqhd,bkhd->bhqk', qb, k_all).astype(jnp.float32) * scale
        p = jax.nn.softmax(s, axis=-1).astype(v_all.dtype)
        return jnp.einsum('bhqk,bkhd->bqhd', p, v_all)

    o = lax.map(blk, jnp.arange(S // BLOCK))
    return o.transpose(1, 0, 2, 3, 4).reshape(B, S, H * MLA_V)


def _swa_latent(q, k, v, k_ctx, v_ctx, sink_l):
    B, S, KV, G, d = q.shape
    C = k_ctx.shape[1]
    scale = d ** -0.5
    span = BLOCK + 2 * WINDOW
    pad = ((0, 0), (WINDOW, WINDOW), (0, 0), (0, 0))
    k_pad = jnp.pad(k, pad)
    v_pad = jnp.pad(v, pad)
    qi = jnp.arange(BLOCK)[:, None]
    kj = jnp.arange(span)[None, :]
    band = jnp.abs(kj - WINDOW - qi) <= WINDOW
    sink_b = jnp.broadcast_to(sink_l.astype(jnp.float32).reshape(1, KV, G, 1, 1), (B, KV, G, BLOCK, 1))

    def blk(i):
        start = i * BLOCK
        qb = lax.dynamic_slice_in_dim(q, start, BLOCK, axis=1)
        kw = lax.dynamic_slice_in_dim(k_pad, start, span, axis=1)
        vw = lax.dynamic_slice_in_dim(v_pad, start, span, axis=1)
        pos = start - WINDOW + kj
        mask = band & (pos >= 0) & (pos < S)
        s_w = jnp.einsum('bqkgd,bpkd->bkgqp', qb, kw).astype(jnp.float32) * scale
        s_w = jnp.where(mask, s_w, NEG)
        s_c = jnp.einsum('bqkgd,bckd->bkgqc', qb, k_ctx).astype(jnp.float32) * scale
        p = jax.nn.softmax(jnp.concatenate([s_w, s_c, sink_b], axis=-1), axis=-1).astype(v.dtype)
        return (jnp.einsum('bkgqp,bpkd->bqkgd', p[..., :span], vw)
                + jnp.einsum('bkgqc,bckd->bqkgd', p[..., span:span + C], v_ctx))

    o = lax.map(blk, jnp.arange(S // BLOCK))
    return o.transpose(1, 0, 2, 3, 4, 5).reshape(B, S, KV * G * d)


def _context_mixer(cx, sink_l, w_ba, w_bb, w_o):
    q, k_nope, v, k_pe, gate_a, q_s, k_s, v_s, gate_b, mg_a, mg_b = cx
    B, C = q.shape[:2]
    k = _mla_keys(k_nope, k_pe)
    s = jnp.einsum('bqhd,bkhd->bhqk', q, k).astype(jnp.float32) * (q.shape[-1] ** -0.5)
    a = jnp.einsum('bhqk,bkhd->bqhd', jax.nn.softmax(s, axis=-1).astype(v.dtype), v).reshape(B, C, MLA_WIDTH)
    s2 = jnp.einsum('bqkgd,bckd->bkgqc', q_s, k_s).astype(jnp.float32) * (SWA_HEAD_DIM ** -0.5)
    sink_b = jnp.broadcast_to(sink_l.astype(jnp.float32).reshape(1, SWA_KV_HEADS, SWA_GROUP, 1, 1),
                              s2.shape[:-1] + (1,))
    p2 = jax.nn.softmax(jnp.concatenate([s2, sink_b], axis=-1), axis=-1)[..., :C].astype(v_s.dtype)
    b = jnp.einsum('bkgqc,bckd->bqkgd', p2, v_s).reshape(B, C, SWA_WIDTH)
    return _merge(a, gate_a, b, gate_b, mg_a, mg_b, w_ba, w_bb, w_o)


def setup_inputs(seed: int = 0) -> dict:
    key = jax.random.key(seed)
    ks = jax.random.split(key, 20)
    D = D_MODEL

    def nrm(k, shape, fan_in):
        return jax.random.normal(k, shape, jnp.float32) * (fan_in ** -0.5)

    return {
        "x": jax.random.normal(ks[0], (BATCH, SEQ, D), jnp.float32),
        "c": jax.random.normal(ks[1], (BATCH, D), jnp.float32),
        "ctx": jax.random.normal(ks[2], (BATCH, CTX_LEN, D), jnp.float32),
        "c_ctx": jax.random.normal(ks[3], (D,), jnp.float32),
        "w_ada": nrm(ks[4], (DEPTH, D, 3 * D), D),
        "b_ada": 0.02 * jax.random.normal(ks[5], (DEPTH, 3 * D), jnp.float32),
        "norm_g": 1.0 + 0.02 * jax.random.normal(ks[6], (DEPTH, D), jnp.float32),
        "w_in": nrm(ks[7], (DEPTH, D, IN_WIDTH), D),
        "b_in": 0.02 * jax.random.normal(ks[8], (DEPTH, IN_WIDTH), jnp.float32),
        "q_norm_g": 1.0 + 0.02 * jax.random.normal(ks[9], (DEPTH, MLA_Q_RANK), jnp.float32),
        "kv_norm_g": 1.0 + 0.02 * jax.random.normal(ks[10], (DEPTH, MLA_KV_RANK), jnp.float32),
        "w_uq": nrm(ks[11], (DEPTH, MLA_Q_RANK, MLA_HEADS * (MLA_NOPE + MLA_ROPE)), MLA_Q_RANK),
        "w_ukv": nrm(ks[12], (DEPTH, MLA_KV_RANK, MLA_HEADS * (MLA_NOPE + MLA_V)), MLA_KV_RANK),
        "sink": jax.random.normal(ks[13], (DEPTH, SWA_HEADS), jnp.float32),
        "w_branch_a": nrm(ks[14], (DEPTH, MLA_WIDTH, D), MLA_WIDTH),
        "w_branch_b": nrm(ks[15], (DEPTH, SWA_WIDTH, D), SWA_WIDTH),
        "w_out": nrm(ks[16], (DEPTH, D, D), D),
        "final_g": 1.0 + 0.02 * jax.random.normal(ks[17], (D,), jnp.float32),
    }


def reference(x, c, ctx, c_ctx, w_ada, b_ada, norm_g, w_in, b_in, q_norm_g, kv_norm_g,
              w_uq, w_ukv, sink, w_branch_a, w_branch_b, w_out, final_g):
    B, S, D = x.shape
    ROWS = S // GRID_W
    rows = jnp.repeat(jnp.arange(ROWS, dtype=jnp.int32), GRID_W)
    cols = jnp.tile(jnp.arange(GRID_W, dtype=jnp.int32), ROWS)
    s_c = jax.nn.silu(c)
    s_cc = jax.nn.silu(c_ctx)
    for l in range(DEPTH):
        shift, scale, gate = jnp.split((s_c @ w_ada[l] + b_ada[l])[:, None, :], 3, axis=-1)
        shift_c, scale_c, gate_c = jnp.split(s_cc @ w_ada[l] + b_ada[l], 3)
        h = _rmsnorm(x, norm_g[l]) * (1 + scale) + shift
        h_c = _rmsnorm(ctx, norm_g[l]) * (1 + scale_c) + shift_c

        lat = _branch_inputs(h, w_in[l], b_in[l], q_norm_g[l], kv_norm_g[l], w_uq[l], w_ukv[l])
        cx = _branch_inputs(h_c, w_in[l], b_in[l], q_norm_g[l], kv_norm_g[l], w_uq[l], w_ukv[l])
        q_mla, k_nope, v_mla, k_pe, gate_a, q_s, k_s, v_s, gate_b, mg_a, mg_b = lat

        q_mla = jnp.concatenate([q_mla[..., :MLA_NOPE], _rope_2d(q_mla[..., MLA_NOPE:], rows, cols)], axis=-1)
        k_mla = _mla_keys(k_nope, _rope_2d(k_pe, rows, cols))
        q_s = _rope_2d(q_s, rows, cols)
        k_s = _rope_2d(k_s, rows, cols)

        k_mla_ctx = _mla_keys(cx[1], cx[3])
        a_heads = _mla_latent(q_mla, k_mla, v_mla, k_mla_ctx, cx[2])
        b_heads = _swa_latent(q_s, k_s, v_s, cx[6], cx[7], sink[l])
        out = _merge(a_heads, gate_a, b_heads, gate_b, mg_a, mg_b, w_branch_a[l], w_branch_b[l], w_out[l])
        if l + 1 < DEPTH:
            ctx = ctx + gate_c * _context_mixer(cx, sink[l], w_branch_a[l], w_branch_b[l], w_out[l])
        x = x + gate * out
    return _rmsnorm(x, final_g)
```

```python
import functools

import jax
import jax.numpy as jnp
from jax import lax
from jax.experimental import pallas as pl
from jax.experimental.pallas import tpu as pltpu

F32 = jnp.float32
BF16 = jnp.bfloat16

LANE = 128
EPS = 1e-6
NEG = -1e30
ROPE_BASE = 10000.0
GRID_W = 64
WINDOW = 128

MLA_HEADS = 8
MLA_NOPE = 128
MLA_ROPE = 64
MLA_V = 128
MLA_QK = MLA_NOPE + MLA_ROPE
MLA_QK_PAD = 2 * LANE
MLA_RANK = 512
SWA_HEADS = 16
SWA_KV_HEADS = 4
SWA_GROUP = SWA_HEADS // SWA_KV_HEADS
SWA_DIM = 64

VMEM_LIMIT = 56 * 1024 * 1024


def _params(*sem):
    return pltpu.CompilerParams(dimension_semantics=sem, vmem_limit_bytes=VMEM_LIMIT)


def _sigmoid(x):
    return 1.0 / (1.0 + jnp.exp(-x))


def _dot(a, b):
    return jnp.dot(a, b, preferred_element_type=F32)


def _dot_nt(a, b):
    return lax.dot_general(a, b, (((1,), (1,)), ((), ())), preferred_element_type=F32)


def _rope_tile(x, cos, sin_signed):
    lane = lax.broadcasted_iota(jnp.int32, x.shape, 1)
    first = (lane % 32) < 16
    partner = jnp.where(first, pltpu.roll(x, LANE - 16, 1), pltpu.roll(x, 16, 1))
    return x * cos + partner * sin_signed


def _split_bf16(a):
    hi = a.astype(BF16)
    lo = (a - hi.astype(F32)).astype(BF16)
    return hi, lo


def _mod_kernel(c_ref, w_ref, b_ref, o_ref):
    c = c_ref[...]
    s = c * _sigmoid(c)
    s_hi, s_lo = _split_bf16(s)
    w_hi, w_lo = _split_bf16(w_ref[...])
    acc = _dot(s_hi, w_hi) + (_dot(s_hi, w_lo) + _dot(s_lo, w_hi))
    o_ref[...] = acc + b_ref[...]


def _modulation(c_rows, w_ada, b_ada):
    rows, d = c_rows.shape
    n = w_ada.shape[1]
    tn = 1536
    return pl.pallas_call(
        _mod_kernel,
        grid=(n // tn,),
        in_specs=[pl.BlockSpec((rows, d), lambda j: (0, 0)),
                  pl.BlockSpec((d, tn), lambda j: (0, j)),
                  pl.BlockSpec((1, tn), lambda j: (0, j))],
        out_specs=pl.BlockSpec((rows, tn), lambda j: (0, j)),
        out_shape=jax.ShapeDtypeStruct((rows, n), F32),
        compiler_params=_params("arbitrary"),
        name="adaln_mod",
    )(c_rows, w_ada, b_ada.reshape(1, n))


def _norm_mod_kernel(x_ref, g_ref, shift_ref, scale_ref, h_ref):
    x = x_ref[0]
    y = x * lax.rsqrt(jnp.mean(x * x, axis=-1, keepdims=True) + EPS)
    h = (y * g_ref[...]) * (1.0 + scale_ref[0]) + shift_ref[0]
    h_ref[0] = h.astype(h_ref.dtype)


def _norm_mod(x, g, mod, tm):
    b, s, d = x.shape
    return pl.pallas_call(
        _norm_mod_kernel,
        grid=(b, s // tm),
        in_specs=[pl.BlockSpec((1, tm, d), lambda bi, i: (bi, i, 0)),
                  pl.BlockSpec((1, d), lambda bi, i: (0, 0)),
                  pl.BlockSpec((1, 1, d), lambda bi, i: (bi, 0, 0)),
                  pl.BlockSpec((1, 1, d), lambda bi, i: (bi, 0, 1))],
        out_specs=pl.BlockSpec((1, tm, d), lambda bi, i: (bi, i, 0)),
        out_shape=jax.ShapeDtypeStruct((b, s, d), BF16),
        compiler_params=_params("arbitrary", "arbitrary"),
        name="norm_mod",
    )(x, g.reshape(1, d), mod, mod)


def _gates_kernel(h_ref, w_ref, b_ref, o_ref, *, silu_blocks):
    z = _dot(h_ref[0], w_ref[...]) + b_ref[...]
    sg = _sigmoid(z)
    is_silu = pl.program_id(2) < silu_blocks
    o_ref[0] = (sg * jnp.where(is_silu, z, 1.0)).astype(o_ref.dtype)


def _gates(h, w, bias, silu_cols, tm, tn):
    b, s, d = h.shape
    n = w.shape[1]
    return pl.pallas_call(
        functools.partial(_gates_kernel, silu_blocks=silu_cols // tn),
        grid=(b, s // tm, n // tn),
        in_specs=[pl.BlockSpec((1, tm, d), lambda bi, i, j: (bi, i, 0)),
                  pl.BlockSpec((d, tn), lambda bi, i, j: (0, j)),
                  pl.BlockSpec((1, tn), lambda bi, i, j: (0, j))],
        out_specs=pl.BlockSpec((1, tm, tn), lambda bi, i, j: (bi, i, j)),
        out_shape=jax.ShapeDtypeStruct((b, s, n), BF16),
        compiler_params=_params("arbitrary", "arbitrary", "arbitrary"),
        name="gate_proj",
    )(h, w, bias.reshape(1, n))


def _rms(x, g):
    return (x * lax.rsqrt(jnp.mean(x * x, axis=-1, keepdims=True) + EPS)) * g


def _mla_proj_kernel(h_ref, w_ref, b_ref, qg_ref, kvg_ref, wuq_ref, wukv_ref, cos_ref, sin_ref,
                     q_ref, k_ref, v_ref, *, q_scale):
    z = _dot(h_ref[0], w_ref[...]) + b_ref[...]
    cos = cos_ref[...]
    sin = sin_ref[...]
    cq = _rms(z[:, :MLA_RANK], qg_ref[...]).astype(BF16)
    ckv = _rms(z[:, MLA_RANK:2 * MLA_RANK], kvg_ref[...]).astype(BF16)
    k_pe = _rope_tile(z[:, 2 * MLA_RANK:], cos, sin).astype(k_ref.dtype)
    q = _dot(cq, wuq_ref[...])
    kv = _dot(ckv, wukv_ref[...])
    for hd in range(MLA_HEADS):
        base = hd * MLA_QK_PAD
        q_ref[0, :, base:base + LANE] = (q[:, base:base + LANE] * q_scale).astype(q_ref.dtype)
        q_rope = _rope_tile(q[:, base + LANE:base + 2 * LANE], cos, sin)
        q_ref[0, :, base + LANE:base + 2 * LANE] = (q_rope * q_scale).astype(q_ref.dtype)
        k_ref[0, :, base:base + LANE] = kv[:, base:base + LANE].astype(k_ref.dtype)
        k_ref[0, :, base + LANE:base + 2 * LANE] = k_pe
        v_ref[0, :, hd * MLA_V:(hd + 1) * MLA_V] = kv[:, base + LANE:base + 2 * LANE].astype(v_ref.dtype)


def _mla_proj(h, w, bias, qg, kvg, wuq, wukv, cos, sin, tm):
    b, s, d = h.shape
    n = w.shape[1]
    nq = MLA_HEADS * MLA_QK_PAD
    nv = MLA_HEADS * MLA_V
    const = lambda bi, i: (0, 0)
    return pl.pallas_call(
        functools.partial(_mla_proj_kernel, q_scale=float(MLA_QK) ** -0.5),
        grid=(b, s // tm),
        in_specs=[pl.BlockSpec((1, tm, d), lambda bi, i: (bi, i, 0)),
                  pl.BlockSpec((d, n), const),
                  pl.BlockSpec((1, n), const),
                  pl.BlockSpec((1, MLA_RANK), const),
                  pl.BlockSpec((1, MLA_RANK), const),
                  pl.BlockSpec((MLA_RANK, nq), const),
                  pl.BlockSpec((MLA_RANK, nq), const),
                  pl.BlockSpec((tm, LANE), lambda bi, i: (i, 0)),
                  pl.BlockSpec((tm, LANE), lambda bi, i: (i, 0))],
        out_specs=[pl.BlockSpec((1, tm, nq), lambda bi, i: (bi, i, 0)),
                   pl.BlockSpec((1, tm, nq), lambda bi, i: (bi, i, 0)),
                   pl.BlockSpec((1, tm, nv), lambda bi, i: (bi, i, 0))],
        out_shape=[jax.ShapeDtypeStruct((b, s, nq), BF16),
                   jax.ShapeDtypeStruct((b, s, nq), BF16),
                   jax.ShapeDtypeStruct((b, s, nv), BF16)],
        compiler_params=_params("arbitrary", "arbitrary"),
        name="mla_proj",
    )(h, w, bias.reshape(1, n), qg.reshape(1, -1), kvg.reshape(1, -1), wuq, wukv, cos, sin)


def _swa_proj_kernel(h_ref, w_ref, b_ref, cos_ref, sin_ref, q_ref, k_ref, v_ref, *, nq, nk, q_scale):
    z = _dot(h_ref[0], w_ref[...]) + b_ref[...]
    cos = cos_ref[...]
    sin = sin_ref[...]
    for t in range(nq // LANE):
        q_t = _rope_tile(z[:, t * LANE:(t + 1) * LANE], cos, sin) * q_scale
        q_ref[0, :, t * LANE:(t + 1) * LANE] = q_t.astype(q_ref.dtype)
    for t in range(nk // LANE):
        k_t = _rope_tile(z[:, nq + t * LANE:nq + (t + 1) * LANE], cos, sin)
        k_ref[0, :, t * LANE:(t + 1) * LANE] = k_t.astype(k_ref.dtype)
    v_ref[0] = z[:, nq + nk:].astype(v_ref.dtype)


def _swa_proj(h, w, bias, cos, sin, tm):
    b, s, d = h.shape
    n = w.shape[1]
    nq = SWA_HEADS * SWA_DIM
    nk = SWA_KV_HEADS * LANE
    const = lambda bi, i: (0, 0)
    return pl.pallas_call(
        functools.partial(_swa_proj_kernel, nq=nq, nk=nk, q_scale=float(SWA_DIM) ** -0.5),
        grid=(b, s // tm),
        in_specs=[pl.BlockSpec((1, tm, d), lambda bi, i: (bi, i, 0)),
                  pl.BlockSpec((d, n), const),
                  pl.BlockSpec((1, n), const),
                  pl.BlockSpec((tm, LANE), lambda bi, i: (i, 0)),
                  pl.BlockSpec((tm, LANE), lambda bi, i: (i, 0))],
        out_specs=[pl.BlockSpec((1, tm, nq), lambda bi, i: (bi, i, 0)),
                   pl.BlockSpec((1, tm, nk), lambda bi, i: (bi, i, 0)),
                   pl.BlockSpec((1, tm, nk), lambda bi, i: (bi, i, 0))],
        out_shape=[jax.ShapeDtypeStruct((b, s, nq), BF16),
                   jax.ShapeDtypeStruct((b, s, nk), BF16),
                   jax.ShapeDtypeStruct((b, s, nk), BF16)],
        compiler_params=_params("arbitrary", "arbitrary"),
        name="swa_proj",
    )(h, w, bias.reshape(1, n), cos, sin)


def _online_step(q, k, v, m, l, acc):
    s = _dot_nt(q, k)
    m_new = jnp.maximum(m, jnp.max(s, axis=1, keepdims=True))
    alpha = jnp.exp(m - m_new)
    p = jnp.exp(s - m_new)
    l = alpha * l + jnp.sum(p, axis=1, keepdims=True)
    acc = alpha * acc + _dot(p.astype(v.dtype), v)
    return m_new, l, acc


def _mla_attn_kernel(q_ref, k_ref, v_ref, kc_ref, vc_ref, o_ref, *, tk):
    q = q_ref[0]
    tq = q.shape[0]
    n_chunks = k_ref.shape[1] // tk

    def body(j, carry):
        off = pl.multiple_of(j * tk, tk)
        return _online_step(q, k_ref[0, pl.ds(off, tk), :], v_ref[0, pl.ds(off, tk), :], *carry)

    init = (jnp.full((tq, 1), -jnp.inf, F32), jnp.zeros((tq, 1), F32), jnp.zeros((tq, MLA_V), F32))
    carry = lax.fori_loop(0, n_chunks, body, init)
    m, l, acc = _online_step(q, kc_ref[0], vc_ref[0], *carry)
    o_ref[0] = (acc / l).astype(o_ref.dtype)


def _mla_attn(q, k, v, kc, vc, tq, tk):
    b, s, _ = q.shape
    c = kc.shape[1]
    return pl.pallas_call(
        functools.partial(_mla_attn_kernel, tk=tk),
        grid=(b, MLA_HEADS, s // tq),
        in_specs=[pl.BlockSpec((1, tq, MLA_QK_PAD), lambda bi, hd, i: (bi, i, hd)),
                  pl.BlockSpec((1, s, MLA_QK_PAD), lambda bi, hd, i: (bi, 0, hd)),
                  pl.BlockSpec((1, s, MLA_V), lambda bi, hd, i: (bi, 0, hd)),
                  pl.BlockSpec((1, c, MLA_QK_PAD), lambda bi, hd, i: (bi, 0, hd)),
                  pl.BlockSpec((1, c, MLA_V), lambda bi, hd, i: (bi, 0, hd))],
        out_specs=pl.BlockSpec((1, tq, MLA_V), lambda bi, hd, i: (bi, i, hd)),
        out_shape=jax.ShapeDtypeStruct((b, s, MLA_HEADS * MLA_V), BF16),
        compiler_params=_params("arbitrary", "arbitrary", "arbitrary"),
        name="mla_attn",
    )(q, k, v, kc, vc)


def _swa_attn_kernel(sink_ref, q_ref, k_ref, v_ref, kc_ref, vc_ref, o_ref, *, tq, span):
    kvh = pl.program_id(1)
    i = pl.program_id(2)
    s_len = k_ref.shape[1]
    q0 = i * tq
    start = pl.multiple_of(jnp.clip(q0 - WINDOW, 0, s_len - span), LANE)
    kw = k_ref[0, pl.ds(start, span), :]
    vw = v_ref[0, pl.ds(start, span), :]
    kc = kc_ref[0]
    vc = vc_ref[0]
    qpos = q0 + lax.broadcasted_iota(jnp.int32, (tq, span), 0)
    kpos = start + lax.broadcasted_iota(jnp.int32, (tq, span), 1)
    band = jnp.abs(qpos - kpos) <= WINDOW
    low_half = lax.broadcasted_iota(jnp.int32, (tq, LANE), 1) < SWA_DIM
    outs = []
    for g in range(SWA_GROUP):
        q_t = q_ref[0, :, (g // 2) * LANE:(g // 2 + 1) * LANE]
        keep = low_half if g % 2 == 0 else jnp.logical_not(low_half)
        q_h = jnp.where(keep, q_t, jnp.zeros_like(q_t))
        s_w = jnp.where(band, _dot_nt(q_h, kw), NEG)
        s_c = _dot_nt(q_h, kc)
        sink = sink_ref[kvh * SWA_GROUP + g]
        m = jnp.maximum(jnp.maximum(jnp.max(s_w, axis=1, keepdims=True),
                                    jnp.max(s_c, axis=1, keepdims=True)), sink)
        p_w = jnp.exp(s_w - m)
        p_c = jnp.exp(s_c - m)
        l = jnp.sum(p_w, axis=1, keepdims=True) + jnp.sum(p_c, axis=1, keepdims=True) + jnp.exp(sink - m)
        o = _dot(p_w.astype(vw.dtype), vw) + _dot(p_c.astype(vc.dtype), vc)
        outs.append(o / l)
    for t in range(SWA_GROUP // 2):
        o_ref[0, :, t * LANE:(t + 1) * LANE] = jnp.where(low_half, outs[2 * t], outs[2 * t + 1]).astype(o_ref.dtype)


def _swa_attn(sink, q, k, v, kc, vc, tq):
    b, s, _ = q.shape
    c = kc.shape[1]
    span = tq + 2 * WINDOW
    gw = SWA_GROUP * SWA_DIM
    return pl.pallas_call(
        functools.partial(_swa_attn_kernel, tq=tq, span=span),
        grid=(b, SWA_KV_HEADS, s // tq),
        in_specs=[pl.BlockSpec(memory_space=pltpu.SMEM),
                  pl.BlockSpec((1, tq, gw), lambda bi, kh, i: (bi, i, kh)),
                  pl.BlockSpec((1, s, LANE), lambda bi, kh, i: (bi, 0, kh)),
                  pl.BlockSpec((1, s, LANE), lambda bi, kh, i: (bi, 0, kh)),
                  pl.BlockSpec((1, c, LANE), lambda bi, kh, i: (bi, 0, kh)),
                  pl.BlockSpec((1, c, LANE), lambda bi, kh, i: (bi, 0, kh))],
        out_specs=pl.BlockSpec((1, tq, gw), lambda bi, kh, i: (bi, i, kh)),
        out_shape=jax.ShapeDtypeStruct((b, s, SWA_HEADS * SWA_DIM), BF16),
        compiler_params=_params("arbitrary", "arbitrary", "arbitrary"),
        name="swa_attn",
    )(sink, q, k, v, kc, vc)


def _merge_kernel(a_ref, b_ref, ga_ref, gb_ref, mga_ref, mgb_ref, x_ref, gate_ref,
                  wba_ref, wbb_ref, wo_ref, fg_ref, o_ref):
    a_in = (a_ref[0].astype(F32) * ga_ref[0].astype(F32)).astype(BF16)
    b_in = (b_ref[0].astype(F32) * gb_ref[0].astype(F32)).astype(BF16)
    y_a = _dot(a_in, wba_ref[...])
    y_b = _dot(b_in, wbb_ref[...])
    mix = mga_ref[0].astype(F32) * y_a + mgb_ref[0].astype(F32) * y_b
    out = _dot(mix.astype(BF16), wo_ref[...])
    xn = x_ref[0] + gate_ref[0] * out
    o_ref[0] = _rms(xn, fg_ref[...]).astype(o_ref.dtype)


def _merge(a, bh, gates, x, mod, wba, wbb, wo, fg, tm):
    b, s, d = x.shape
    wa = a.shape[2]
    const = lambda bi, i: (0, 0)
    row = lambda col: (lambda bi, i: (bi, i, col))
    return pl.pallas_call(
        _merge_kernel,
        grid=(b, s // tm),
        in_specs=[pl.BlockSpec((1, tm, wa), row(0)),
                  pl.BlockSpec((1, tm, wa), row(0)),
                  pl.BlockSpec((1, tm, wa), row(0)),
                  pl.BlockSpec((1, tm, wa), row(1)),
                  pl.BlockSpec((1, tm, d), row(1)),
                  pl.BlockSpec((1, tm, d), row(2)),
                  pl.BlockSpec((1, tm, d), row(0)),
                  pl.BlockSpec((1, 1, d), lambda bi, i: (bi, 0, 2)),
                  pl.BlockSpec((wa, d), const),
                  pl.BlockSpec((wa, d), const),
                  pl.BlockSpec((d, d), const),
                  pl.BlockSpec((1, d), const)],
        out_specs=pl.BlockSpec((1, tm, d), row(0)),
        out_shape=jax.ShapeDtypeStruct((b, s, d), x.dtype),
        compiler_params=_params("arbitrary", "arbitrary"),
        name="merge_out",
    )(a, bh, gates, gates, gates, gates, x, mod, wba, wbb, wo, fg.reshape(1, d))


def _rope_tables(s):
    t = jnp.arange(s, dtype=jnp.int32)
    n = 16
    inv = ROPE_BASE ** (-jnp.arange(n, dtype=F32) / n)
    ang_r = (t // GRID_W).astype(F32)[:, None] * inv[None, :]
    ang_c = (t % GRID_W).astype(F32)[:, None] * inv[None, :]
    cos = jnp.concatenate([jnp.cos(ang_r)] * 2 + [jnp.cos(ang_c)] * 2, axis=1)
    sin = jnp.concatenate([-jnp.sin(ang_r), jnp.sin(ang_r), -jnp.sin(ang_c), jnp.sin(ang_c)], axis=1)
    return jnp.tile(cos, (1, 2)), jnp.tile(sin, (1, 2))


def _dup_heads(w, heads, dim):
    lead = w.shape[:-1]
    w = w.reshape(lead + (heads, 1, dim))
    return jnp.broadcast_to(w, lead + (heads, 2, dim)).reshape(lead + (heads * 2 * dim,))


def kernel(x, c, ctx, c_ctx, w_ada, b_ada, norm_g, w_in, b_in, q_norm_g, kv_norm_g, w_uq, w_ukv, sink,
           w_branch_a, w_branch_b, w_out, final_g):
    bsz, s, d = x.shape
    c_len = ctx.shape[1]
    assert w_ada.shape[0] == 1, "single-layer block"
    widths = (MLA_RANK, MLA_RANK, MLA_ROPE, MLA_HEADS * MLA_V, SWA_HEADS * SWA_DIM,
              SWA_KV_HEADS * SWA_DIM, SWA_KV_HEADS * SWA_DIM, SWA_HEADS * SWA_DIM, d, d)
    assert sum(widths) == w_in.shape[2]
    offs = [0]
    for w_ in widths:
        offs.append(offs[-1] + w_)
    w_in0, b_in0 = w_in[0], b_in[0]
    col = lambda k: (w_in0[:, offs[k]:offs[k + 1]], b_in0[offs[k]:offs[k + 1]])
    (w_cq, b_cq), (w_ckv, b_ckv), (w_kpe, b_kpe), (w_ga, b_ga), (w_qs, b_qs) = (col(k) for k in range(5))
    (w_ks, b_ks), (w_vs, b_vs), (w_gb, b_gb), (w_mga, b_mga), (w_mgb, b_mgb) = (col(k) for k in range(5, 10))

    rows = 8
    c_rows = jnp.concatenate([c, c_ctx[None, :], jnp.zeros((rows - bsz - 1, d), F32)], axis=0)
    mod = _modulation(c_rows, w_ada[0], b_ada[0])
    mod_lat = mod[:bsz].reshape(bsz, 1, 3 * d)
    mod_ctx = jnp.broadcast_to(mod[bsz].reshape(1, 1, 3 * d), (bsz, 1, 3 * d))

    h = _norm_mod(x, norm_g[0], mod_lat, 512)
    h_c = _norm_mod(ctx, norm_g[0], mod_ctx, c_len)

    pad_pe = LANE - MLA_ROPE
    w_mla = jnp.concatenate([w_cq, w_ckv, w_kpe, jnp.zeros((d, pad_pe), F32)], axis=1).astype(BF16)
    b_mla = jnp.concatenate([b_cq, b_ckv, b_kpe, jnp.zeros((pad_pe,), F32)])
    wuq = jnp.pad(w_uq[0].reshape(MLA_RANK, MLA_HEADS, MLA_QK),
                  ((0, 0), (0, 0), (0, MLA_QK_PAD - MLA_QK))).reshape(MLA_RANK, -1).astype(BF16)
    wukv = w_ukv[0].astype(BF16)
    w_swa = jnp.concatenate([w_qs, _dup_heads(w_ks, SWA_KV_HEADS, SWA_DIM),
                             _dup_heads(w_vs, SWA_KV_HEADS, SWA_DIM)], axis=1).astype(BF16)
    b_swa = jnp.concatenate([b_qs, _dup_heads(b_ks, SWA_KV_HEADS, SWA_DIM),
                             _dup_heads(b_vs, SWA_KV_HEADS, SWA_DIM)])
    w_gates = jnp.concatenate([w_ga, w_gb, w_mga, w_mgb], axis=1).astype(BF16)
    b_gates = jnp.concatenate([b_ga, b_gb, b_mga, b_mgb])

    cos, sin = _rope_tables(s)
    cos_c = jnp.ones((c_len, LANE), F32)
    sin_c = jnp.zeros((c_len, LANE), F32)

    q_mla, k_mla, v_mla = _mla_proj(h, w_mla, b_mla, q_norm_g[0], kv_norm_g[0], wuq, wukv, cos, sin, 512)
    _, kc_mla, vc_mla = _mla_proj(h_c, w_mla, b_mla, q_norm_g[0], kv_norm_g[0], wuq, wukv, cos_c, sin_c, c_len)
    q_s, k_s, v_s = _swa_proj(h, w_swa, b_swa, cos, sin, 512)
    _, kc_s, vc_s = _swa_proj(h_c, w_swa, b_swa, cos_c, sin_c, c_len)
    gates = _gates(h, w_gates, b_gates, 2 * MLA_HEADS * MLA_V, 1024, 1024)

    a_heads = _mla_attn(q_mla, k_mla, v_mla, kc_mla, vc_mla, 512, 1024)
    b_heads = _swa_attn(sink[0], q_s, k_s, v_s, kc_s, vc_s, 256)

    return _merge(a_heads, b_heads, gates, x, mod_lat, w_branch_a[0].astype(BF16),
                  w_branch_b[0].astype(BF16), w_out[0].astype(BF16), final_g, 256)
```

```python
import functools

import jax
import jax.numpy as jnp
from jax import lax
from jax.experimental import pallas as pl
from jax.experimental.pallas import tpu as pltpu

F32 = jnp.float32
BF16 = jnp.bfloat16

LANE = 128
EPS = 1e-6
NEG = -1e30
ROPE_BASE = 10000.0
GRID_W = 64
LOG2E = 1.4426950408889634
WINDOW = 128

MLA_HEADS = 8
MLA_NOPE = 128
MLA_ROPE = 64
MLA_V = 128
MLA_QK = MLA_NOPE + MLA_ROPE
MLA_QK_PAD = 2 * LANE
MLA_RANK = 512
SWA_HEADS = 16
SWA_KV_HEADS = 4
SWA_GROUP = SWA_HEADS // SWA_KV_HEADS
SWA_DIM = 64

VMEM_LIMIT = 56 * 1024 * 1024


def _params(*sem):
    return pltpu.CompilerParams(dimension_semantics=sem, vmem_limit_bytes=VMEM_LIMIT)


def _sigmoid(x):
    return 1.0 / (1.0 + jnp.exp(-x))


def _dot(a, b):
    return jnp.dot(a, b, preferred_element_type=F32)


def _dot_nt(a, b):
    return lax.dot_general(a, b, (((1,), (1,)), ((), ())), preferred_element_type=F32)


def _rope_tile(x, cos, sin_signed):
    lane = lax.broadcasted_iota(jnp.int32, x.shape, 1)
    first = (lane % 32) < 16
    partner = jnp.where(first, pltpu.roll(x, LANE - 16, 1), pltpu.roll(x, 16, 1))
    return x * cos + partner * sin_signed


def _split_bf16(a):
    hi = a.astype(BF16)
    lo = (a - hi.astype(F32)).astype(BF16)
    return hi, lo


def _mod_kernel(c_ref, w_ref, b_ref, o_ref):
    c = c_ref[...]
    s = c * _sigmoid(c)
    s_hi, s_lo = _split_bf16(s)
    w_hi, w_lo = _split_bf16(w_ref[...])
    acc = _dot(s_hi, w_hi) + (_dot(s_hi, w_lo) + _dot(s_lo, w_hi))
    o_ref[...] = acc + b_ref[...]


def _modulation(c_rows, w_ada, b_ada):
    rows, d = c_rows.shape
    n = w_ada.shape[1]
    tn = 1536
    return pl.pallas_call(
        _mod_kernel,
        grid=(n // tn,),
        in_specs=[pl.BlockSpec((rows, d), lambda j: (0, 0)),
                  pl.BlockSpec((d, tn), lambda j: (0, j)),
                  pl.BlockSpec((1, tn), lambda j: (0, j))],
        out_specs=pl.BlockSpec((rows, tn), lambda j: (0, j)),
        out_shape=jax.ShapeDtypeStruct((rows, n), F32),
        compiler_params=_params("arbitrary"),
        name="adaln_mod",
    )(c_rows, w_ada, b_ada.reshape(1, n))


def _norm_mod_kernel(x_ref, g_ref, shift_ref, scale_ref, h_ref):
    x = x_ref[0]
    y = x * lax.rsqrt(jnp.mean(x * x, axis=-1, keepdims=True) + EPS)
    h = (y * g_ref[...]) * (1.0 + scale_ref[0]) + shift_ref[0]
    h_ref[0] = h.astype(h_ref.dtype)


def _norm_mod(x, g, mod, tm):
    b, s, d = x.shape
    return pl.pallas_call(
        _norm_mod_kernel,
        grid=(b, s // tm),
        in_specs=[pl.BlockSpec((1, tm, d), lambda bi, i: (bi, i, 0)),
                  pl.BlockSpec((1, d), lambda bi, i: (0, 0)),
                  pl.BlockSpec((1, 1, d), lambda bi, i: (bi, 0, 0)),
                  pl.BlockSpec((1, 1, d), lambda bi, i: (bi, 0, 1))],
        out_specs=pl.BlockSpec((1, tm, d), lambda bi, i: (bi, i, 0)),
        out_shape=jax.ShapeDtypeStruct((b, s, d), BF16),
        compiler_params=_params("arbitrary", "arbitrary"),
        name="norm_mod",
    )(x, g.reshape(1, d), mod, mod)


def _gates_kernel(h_ref, w_ref, b_ref, o_ref, *, silu_blocks):
    z = _dot(h_ref[0], w_ref[...]) + b_ref[...]
    sg = _sigmoid(z)
    is_silu = pl.program_id(2) < silu_blocks
    o_ref[0] = (sg * jnp.where(is_silu, z, 1.0)).astype(o_ref.dtype)


def _gates(h, w, bias, silu_cols, tm, tn):
    b, s, d = h.shape
    n = w.shape[1]
    return pl.pallas_call(
        functools.partial(_gates_kernel, silu_blocks=silu_cols // tn),
        grid=(b, s // tm, n // tn),
        in_specs=[pl.BlockSpec((1, tm, d), lambda bi, i, j: (bi, i, 0)),
                  pl.BlockSpec((d, tn), lambda bi, i, j: (0, j)),
                  pl.BlockSpec((1, tn), lambda bi, i, j: (0, j))],
        out_specs=pl.BlockSpec((1, tm, tn), lambda bi, i, j: (bi, i, j)),
        out_shape=jax.ShapeDtypeStruct((b, s, n), BF16),
        compiler_params=_params("arbitrary", "arbitrary", "arbitrary"),
        name="gate_proj",
    )(h, w, bias.reshape(1, n))


def _rms(x, g):
    return (x * lax.rsqrt(jnp.mean(x * x, axis=-1, keepdims=True) + EPS)) * g


def _mla_proj_kernel(h_ref, w_ref, b_ref, qg_ref, kvg_ref, wuq_ref, wukv_ref, cos_ref, sin_ref,
                     q_ref, k_ref, vt_ref, *, q_scale):
    z = _dot(h_ref[0], w_ref[...]) + b_ref[...]
    cos = cos_ref[...]
    sin = sin_ref[...]
    cq = _rms(z[:, :MLA_RANK], qg_ref[...]).astype(BF16)
    ckv = _rms(z[:, MLA_RANK:2 * MLA_RANK], kvg_ref[...]).astype(BF16)
    k_pe = _rope_tile(z[:, 2 * MLA_RANK:], cos, sin).astype(k_ref.dtype)
    q = _dot(cq, wuq_ref[...])
    kv = _dot(ckv, wukv_ref[...])
    for hd in range(MLA_HEADS):
        base = hd * MLA_QK_PAD
        q_ref[0, :, base:base + LANE] = (q[:, base:base + LANE] * q_scale).astype(q_ref.dtype)
        q_rope = _rope_tile(q[:, base + LANE:base + 2 * LANE], cos, sin)
        q_ref[0, :, base + LANE:base + 2 * LANE] = (q_rope * q_scale).astype(q_ref.dtype)
        k_ref[0, :, base:base + LANE] = kv[:, base:base + LANE].astype(k_ref.dtype)
        k_ref[0, :, base + LANE:base + 2 * LANE] = k_pe
        vt_ref[0, 0, hd * MLA_V:(hd + 1) * MLA_V, :] = kv[:, base + LANE:base + 2 * LANE].T.astype(vt_ref.dtype)


def _mla_proj(h, w, bias, qg, kvg, wuq, wukv, cos, sin, tm, tkc):
    b, s, d = h.shape
    n = w.shape[1]
    nq = MLA_HEADS * MLA_QK_PAD
    nv = MLA_HEADS * MLA_V
    per_chunk = tkc // tm
    const = lambda bi, i: (0, 0)
    return pl.pallas_call(
        functools.partial(_mla_proj_kernel, q_scale=float(MLA_QK) ** -0.5 * LOG2E),
        grid=(b, s // tm),
        in_specs=[pl.BlockSpec((1, tm, d), lambda bi, i: (bi, i, 0)),
                  pl.BlockSpec((d, n), const),
                  pl.BlockSpec((1, n), const),
                  pl.BlockSpec((1, MLA_RANK), const),
                  pl.BlockSpec((1, MLA_RANK), const),
                  pl.BlockSpec((MLA_RANK, nq), const),
                  pl.BlockSpec((MLA_RANK, nq), const),
                  pl.BlockSpec((tm, LANE), lambda bi, i: (i, 0)),
                  pl.BlockSpec((tm, LANE), lambda bi, i: (i, 0))],
        out_specs=[pl.BlockSpec((1, tm, nq), lambda bi, i: (bi, i, 0)),
                   pl.BlockSpec((1, tm, nq), lambda bi, i: (bi, i, 0)),
                   pl.BlockSpec((1, 1, nv, tm), lambda bi, i: (bi, i // per_chunk, 0, i % per_chunk))],
        out_shape=[jax.ShapeDtypeStruct((b, s, nq), BF16),
                   jax.ShapeDtypeStruct((b, s, nq), BF16),
                   jax.ShapeDtypeStruct((b, s // tkc, nv, tkc), BF16)],
        compiler_params=_params("arbitrary", "arbitrary"),
        name="mla_proj",
    )(h, w, bias.reshape(1, n), qg.reshape(1, -1), kvg.reshape(1, -1), wuq, wukv, cos, sin)


def _swa_proj_kernel(h_ref, w_ref, b_ref, cos_ref, sin_ref, q_ref, k_ref, v_ref, *, nq, nk, q_scale):
    z = _dot(h_ref[0], w_ref[...]) + b_ref[...]
    cos = cos_ref[...]
    sin = sin_ref[...]
    for t in range(nq // LANE):
        q_t = _rope_tile(z[:, t * LANE:(t + 1) * LANE], cos, sin) * q_scale
        q_ref[0, :, t * LANE:(t + 1) * LANE] = q_t.astype(q_ref.dtype)
    for t in range(nk // LANE):
        k_t = _rope_tile(z[:, nq + t * LANE:nq + (t + 1) * LANE], cos, sin)
        k_ref[0, :, t * LANE:(t + 1) * LANE] = k_t.astype(k_ref.dtype)
    v_ref[0] = z[:, nq + nk:].astype(v_ref.dtype)


def _swa_proj(h, w, bias, cos, sin, tm):
    b, s, d = h.shape
    n = w.shape[1]
    nq = SWA_HEADS * SWA_DIM
    nk = SWA_KV_HEADS * LANE
    const = lambda bi, i: (0, 0)
    return pl.pallas_call(
        functools.partial(_swa_proj_kernel, nq=nq, nk=nk, q_scale=float(SWA_DIM) ** -0.5),
        grid=(b, s // tm),
        in_specs=[pl.BlockSpec((1, tm, d), lambda bi, i: (bi, i, 0)),
                  pl.BlockSpec((d, n), const),
                  pl.BlockSpec((1, n), const),
                  pl.BlockSpec((tm, LANE), lambda bi, i: (i, 0)),
                  pl.BlockSpec((tm, LANE), lambda bi, i: (i, 0))],
        out_specs=[pl.BlockSpec((1, tm, nq), lambda bi, i: (bi, i, 0)),
                   pl.BlockSpec((1, tm, nk), lambda bi, i: (bi, i, 0)),
                   pl.BlockSpec((1, tm, nk), lambda bi, i: (bi, i, 0))],
        out_shape=[jax.ShapeDtypeStruct((b, s, nq), BF16),
                   jax.ShapeDtypeStruct((b, s, nk), BF16),
                   jax.ShapeDtypeStruct((b, s, nk), BF16)],
        compiler_params=_params("arbitrary", "arbitrary"),
        name="swa_proj",
    )(h, w, bias.reshape(1, n), cos, sin)


def _mla_attn_kernel(q_ref, k_ref, vt_ref, kc_ref, vtc_ref, o_ref, s_a, s_b, p_a, p_b, acc_ref, *, tk, sub):
    tq = q_ref.shape[1]
    n_lat = k_ref.shape[1] // tk
    c = kc_ref.shape[1]
    s_bufs = (s_a, s_b)
    p_bufs = (p_a, p_b)

    def fused_step(m, l, score=None, soft=None, accum=None):
        if soft is not None:
            m_new = jnp.maximum(m, soft[3])
            alpha_new = jnp.exp2(m - m_new)
        cmax = lsum = pv = None
        keys = max(job[2] for job in (score, soft, accum) if job is not None)
        for lo in range(0, keys, sub):
            piece = lambda job: slice(lo, min(lo + sub, job[2]))
            if score is not None and lo < score[2]:
                rows = piece(score)
                s = _dot_nt(score[0](rows), q_ref[0])
                score[1][rows, :] = s
                piece_max = jnp.max(s, axis=0, keepdims=True)
                cmax = piece_max if cmax is None else jnp.maximum(cmax, piece_max)
            if soft is not None and lo < soft[2]:
                rows = piece(soft)
                p = jnp.exp2(soft[0][rows, :] - m_new)
                soft[1][rows, :] = p.astype(soft[1].dtype)
                piece_sum = jnp.sum(p, axis=0, keepdims=True)
                lsum = piece_sum if lsum is None else lsum + piece_sum
            if accum is not None and lo < accum[2]:
                rows = piece(accum)
                d = _dot(accum[0](rows), accum[1][rows, :])
                pv = d if pv is None else pv + d
        if accum is not None:
            acc_ref[...] = accum[3] * acc_ref[...] + pv
        if soft is not None:
            return m_new, alpha_new * l + lsum, alpha_new, cmax
        return m, l, None, cmax

    def k_piece(n):
        return lambda rows: k_ref[0, pl.ds(pl.multiple_of(n * tk, tk) + rows.start, rows.stop - rows.start), :]

    def vt_piece(n):
        return lambda rows: vt_ref[0, n, :, rows]

    acc_ref[...] = jnp.zeros_like(acc_ref)
    m = jnp.full((1, tq), -jnp.inf, F32)
    l = jnp.zeros((1, tq), F32)
    _, _, _, cmax0 = fused_step(m, l, score=(k_piece(0), s_bufs[0], tk))
    m, l, alpha, cmax = fused_step(m, l, score=(k_piece(1), s_bufs[1], tk),
                                   soft=(s_bufs[0], p_bufs[0], tk, cmax0))

    def step(n, u, carry):
        m, l, alpha, cmax = carry
        return fused_step(m, l, score=(k_piece(n), s_bufs[u], tk),
                          soft=(s_bufs[1 - u], p_bufs[1 - u], tk, cmax),
                          accum=(vt_piece(n - 2), p_bufs[u], tk, alpha))

    def pair(t, carry):
        return step(2 * t + 3, 1, step(2 * t + 2, 0, carry))

    m, l, alpha, cmax = lax.fori_loop(0, n_lat // 2 - 1, pair, (m, l, alpha, cmax))
    s_c = s_bufs[0].at[0:c]
    p_c = p_bufs[0].at[0:c]
    m, l, alpha, cmax_c = fused_step(m, l, score=(lambda rows: kc_ref[0, rows, :], s_c, c),
                                     soft=(s_bufs[1], p_bufs[1], tk, cmax),
                                     accum=(vt_piece(n_lat - 2), p_bufs[0], tk, alpha))
    m, l, alpha, _ = fused_step(m, l, soft=(s_c, p_c, c, cmax_c),
                                accum=(vt_piece(n_lat - 1), p_bufs[1], tk, alpha))
    fused_step(m, l, accum=(lambda rows: vtc_ref[0, 0, :, rows], p_c, c, alpha))
    o_ref[0] = (acc_ref[...] / l).T.astype(o_ref.dtype)


def _mla_attn(q, k, vt, kc, vtc, tq, sub):
    b, s, _ = q.shape
    c = kc.shape[1]
    n_chunks, _, tk = vt.shape[1:]
    assert n_chunks % 2 == 0 and n_chunks >= 2 and c <= tk
    return pl.pallas_call(
        functools.partial(_mla_attn_kernel, tk=tk, sub=sub),
        grid=(b, MLA_HEADS, s // tq),
        in_specs=[pl.BlockSpec((1, tq, MLA_QK_PAD), lambda bi, hd, i: (bi, i, hd)),
                  pl.BlockSpec((1, s, MLA_QK_PAD), lambda bi, hd, i: (bi, 0, hd)),
                  pl.BlockSpec((1, n_chunks, MLA_V, tk), lambda bi, hd, i: (bi, 0, hd, 0)),
                  pl.BlockSpec((1, c, MLA_QK_PAD), lambda bi, hd, i: (bi, 0, hd)),
                  pl.BlockSpec((1, 1, MLA_V, c), lambda bi, hd, i: (bi, 0, hd, 0))],
        out_specs=pl.BlockSpec((1, tq, MLA_V), lambda bi, hd, i: (bi, i, hd)),
        out_shape=jax.ShapeDtypeStruct((b, s, MLA_HEADS * MLA_V), BF16),
        scratch_shapes=[pltpu.VMEM((tk, tq), F32), pltpu.VMEM((tk, tq), F32),
                        pltpu.VMEM((tk, tq), BF16), pltpu.VMEM((tk, tq), BF16),
                        pltpu.VMEM((MLA_V, tq), F32)],
        compiler_params=_params("arbitrary", "arbitrary", "arbitrary"),
        name="mla_attn",
    )(q, k, vt, kc, vtc)


def _swa_attn_kernel(sink_ref, q_ref, k_ref, v_ref, kc_ref, vc_ref, o_ref, *, tq, span):
    kvh = pl.program_id(1)
    i = pl.program_id(2)
    s_len = k_ref.shape[1]
    q0 = i * tq
    start = pl.multiple_of(jnp.clip(q0 - WINDOW, 0, s_len - span), LANE)
    kw = k_ref[0, pl.ds(start, span), :]
    vw = v_ref[0, pl.ds(start, span), :]
    kc = kc_ref[0]
    vc = vc_ref[0]
    qpos = q0 + lax.broadcasted_iota(jnp.int32, (tq, span), 0)
    kpos = start + lax.broadcasted_iota(jnp.int32, (tq, span), 1)
    band = jnp.abs(qpos - kpos) <= WINDOW
    low_half = lax.broadcasted_iota(jnp.int32, (tq, LANE), 1) < SWA_DIM
    outs = []
    for g in range(SWA_GROUP):
        q_t = q_ref[0, :, (g // 2) * LANE:(g // 2 + 1) * LANE]
        keep = low_half if g % 2 == 0 else jnp.logical_not(low_half)
        q_h = jnp.where(keep, q_t, jnp.zeros_like(q_t))
        s_w = jnp.where(band, _dot_nt(q_h, kw), NEG)
        s_c = _dot_nt(q_h, kc)
        sink = sink_ref[kvh * SWA_GROUP + g]
        m = jnp.maximum(jnp.maximum(jnp.max(s_w, axis=1, keepdims=True),
                                    jnp.max(s_c, axis=1, keepdims=True)), sink)
        p_w = jnp.exp(s_w - m)
        p_c = jnp.exp(s_c - m)
        l = jnp.sum(p_w, axis=1, keepdims=True) + jnp.sum(p_c, axis=1, keepdims=True) + jnp.exp(sink - m)
        o = _dot(p_w.astype(vw.dtype), vw) + _dot(p_c.astype(vc.dtype), vc)
        outs.append(o / l)
    for t in range(SWA_GROUP // 2):
        o_ref[0, :, t * LANE:(t + 1) * LANE] = jnp.where(low_half, outs[2 * t], outs[2 * t + 1]).astype(o_ref.dtype)


def _swa_attn(sink, q, k, v, kc, vc, tq):
    b, s, _ = q.shape
    c = kc.shape[1]
    span = tq + 2 * WINDOW
    gw = SWA_GROUP * SWA_DIM
    return pl.pallas_call(
        functools.partial(_swa_attn_kernel, tq=tq, span=span),
        grid=(b, SWA_KV_HEADS, s // tq),
        in_specs=[pl.BlockSpec(memory_space=pltpu.SMEM),
                  pl.BlockSpec((1, tq, gw), lambda bi, kh, i: (bi, i, kh)),
                  pl.BlockSpec((1, s, LANE), lambda bi, kh, i: (bi, 0, kh)),
                  pl.BlockSpec((1, s, LANE), lambda bi, kh, i: (bi, 0, kh)),
                  pl.BlockSpec((1, c, LANE), lambda bi, kh, i: (bi, 0, kh)),
                  pl.BlockSpec((1, c, LANE), lambda bi, kh, i: (bi, 0, kh))],
        out_specs=pl.BlockSpec((1, tq, gw), lambda bi, kh, i: (bi, i, kh)),
        out_shape=jax.ShapeDtypeStruct((b, s, SWA_HEADS * SWA_DIM), BF16),
        compiler_params=_params("arbitrary", "arbitrary", "arbitrary"),
        name="swa_attn",
    )(sink, q, k, v, kc, vc)


def _merge_kernel(a_ref, b_ref, ga_ref, gb_ref, mga_ref, mgb_ref, x_ref, gate_ref,
                  wba_ref, wbb_ref, wo_ref, fg_ref, o_ref):
    a_in = (a_ref[0].astype(F32) * ga_ref[0].astype(F32)).astype(BF16)
    b_in = (b_ref[0].astype(F32) * gb_ref[0].astype(F32)).astype(BF16)
    y_a = _dot(a_in, wba_ref[...])
    y_b = _dot(b_in, wbb_ref[...])
    mix = mga_ref[0].astype(F32) * y_a + mgb_ref[0].astype(F32) * y_b
    out = _dot(mix.astype(BF16), wo_ref[...])
    xn = x_ref[0] + gate_ref[0] * out
    o_ref[0] = _rms(xn, fg_ref[...]).astype(o_ref.dtype)


def _merge(a, bh, gates, x, mod, wba, wbb, wo, fg, tm):
    b, s, d = x.shape
    wa = a.shape[2]
    const = lambda bi, i: (0, 0)
    row = lambda col: (lambda bi, i: (bi, i, col))
    return pl.pallas_call(
        _merge_kernel,
        grid=(b, s // tm),
        in_specs=[pl.BlockSpec((1, tm, wa), row(0)),
                  pl.BlockSpec((1, tm, wa), row(0)),
                  pl.BlockSpec((1, tm, wa), row(0)),
                  pl.BlockSpec((1, tm, wa), row(1)),
                  pl.BlockSpec((1, tm, d), row(1)),
                  pl.BlockSpec((1, tm, d), row(2)),
                  pl.BlockSpec((1, tm, d), row(0)),
                  pl.BlockSpec((1, 1, d), lambda bi, i: (bi, 0, 2)),
                  pl.BlockSpec((wa, d), const),
                  pl.BlockSpec((wa, d), const),
                  pl.BlockSpec((d, d), const),
                  pl.BlockSpec((1, d), const)],
        out_specs=pl.BlockSpec((1, tm, d), row(0)),
        out_shape=jax.ShapeDtypeStruct((b, s, d), x.dtype),
        compiler_params=_params("arbitrary", "arbitrary"),
        name="merge_out",
    )(a, bh, gates, gates, gates, gates, x, mod, wba, wbb, wo, fg.reshape(1, d))


def _rope_tables(s):
    t = jnp.arange(s, dtype=jnp.int32)
    n = 16
    inv = ROPE_BASE ** (-jnp.arange(n, dtype=F32) / n)
    ang_r = (t // GRID_W).astype(F32)[:, None] * inv[None, :]
    ang_c = (t % GRID_W).astype(F32)[:, None] * inv[None, :]
    cos = jnp.concatenate([jnp.cos(ang_r)] * 2 + [jnp.cos(ang_c)] * 2, axis=1)
    sin = jnp.concatenate([-jnp.sin(ang_r), jnp.sin(ang_r), -jnp.sin(ang_c), jnp.sin(ang_c)], axis=1)
    return jnp.tile(cos, (1, 2)), jnp.tile(sin, (1, 2))


def _dup_heads(w, heads, dim):
    lead = w.shape[:-1]
    w = w.reshape(lead + (heads, 1, dim))
    return jnp.broadcast_to(w, lead + (heads, 2, dim)).reshape(lead + (heads * 2 * dim,))


def kernel(x, c, ctx, c_ctx, w_ada, b_ada, norm_g, w_in, b_in, q_norm_g, kv_norm_g, w_uq, w_ukv, sink,
           w_branch_a, w_branch_b, w_out, final_g):
    bsz, s, d = x.shape
    c_len = ctx.shape[1]
    assert w_ada.shape[0] == 1, "single-layer block"
    widths = (MLA_RANK, MLA_RANK, MLA_ROPE, MLA_HEADS * MLA_V, SWA_HEADS * SWA_DIM,
              SWA_KV_HEADS * SWA_DIM, SWA_KV_HEADS * SWA_DIM, SWA_HEADS * SWA_DIM, d, d)
    assert sum(widths) == w_in.shape[2]
    offs = [0]
    for w_ in widths:
        offs.append(offs[-1] + w_)
    w_in0, b_in0 = w_in[0], b_in[0]
    col = lambda k: (w_in0[:, offs[k]:offs[k + 1]], b_in0[offs[k]:offs[k + 1]])
    (w_cq, b_cq), (w_ckv, b_ckv), (w_kpe, b_kpe), (w_ga, b_ga), (w_qs, b_qs) = (col(k) for k in range(5))
    (w_ks, b_ks), (w_vs, b_vs), (w_gb, b_gb), (w_mga, b_mga), (w_mgb, b_mgb) = (col(k) for k in range(5, 10))

    rows = 8
    c_rows = jnp.concatenate([c, c_ctx[None, :], jnp.zeros((rows - bsz - 1, d), F32)], axis=0)
    mod = _modulation(c_rows, w_ada[0], b_ada[0])
    mod_lat = mod[:bsz].reshape(bsz, 1, 3 * d)
    mod_ctx = jnp.broadcast_to(mod[bsz].reshape(1, 1, 3 * d), (bsz, 1, 3 * d))

    h = _norm_mod(x, norm_g[0], mod_lat, 512)
    h_c = _norm_mod(ctx, norm_g[0], mod_ctx, c_len)

    pad_pe = LANE - MLA_ROPE
    w_mla = jnp.concatenate([w_cq, w_ckv, w_kpe, jnp.zeros((d, pad_pe), F32)], axis=1).astype(BF16)
    b_mla = jnp.concatenate([b_cq, b_ckv, b_kpe, jnp.zeros((pad_pe,), F32)])
    wuq = jnp.pad(w_uq[0].reshape(MLA_RANK, MLA_HEADS, MLA_QK),
                  ((0, 0), (0, 0), (0, MLA_QK_PAD - MLA_QK))).reshape(MLA_RANK, -1).astype(BF16)
    wukv = w_ukv[0].astype(BF16)
    w_swa = jnp.concatenate([w_qs, _dup_heads(w_ks, SWA_KV_HEADS, SWA_DIM),
                             _dup_heads(w_vs, SWA_KV_HEADS, SWA_DIM)], axis=1).astype(BF16)
    b_swa = jnp.concatenate([b_qs, _dup_heads(b_ks, SWA_KV_HEADS, SWA_DIM),
                             _dup_heads(b_vs, SWA_KV_HEADS, SWA_DIM)])
    w_gates = jnp.concatenate([w_ga, w_gb, w_mga, w_mgb], axis=1).astype(BF16)
    b_gates = jnp.concatenate([b_ga, b_gb, b_mga, b_mgb])

    cos, sin = _rope_tables(s)
    cos_c = jnp.ones((c_len, LANE), F32)
    sin_c = jnp.zeros((c_len, LANE), F32)

    q_mla, k_mla, vt_mla = _mla_proj(h, w_mla, b_mla, q_norm_g[0], kv_norm_g[0], wuq, wukv, cos, sin, 512, 1024)
    _, kc_mla, vtc_mla = _mla_proj(h_c, w_mla, b_mla, q_norm_g[0], kv_norm_g[0], wuq, wukv, cos_c, sin_c,
                                   c_len, c_len)
    q_s, k_s, v_s = _swa_proj(h, w_swa, b_swa, cos, sin, 512)
    _, kc_s, vc_s = _swa_proj(h_c, w_swa, b_swa, cos_c, sin_c, c_len)
    gates = _gates(h, w_gates, b_gates, 2 * MLA_HEADS * MLA_V, 1024, 1024)

    a_heads = _mla_attn(q_mla, k_mla, vt_mla, kc_mla, vtc_mla, 512, 1024)
    b_heads = _swa_attn(sink[0], q_s, k_s, v_s, kc_s, vc_s, 256)

    return _merge(a_heads, b_heads, gates, x, mod_lat, w_branch_a[0].astype(BF16),
                  w_branch_b[0].astype(BF16), w_out[0].astype(BF16), final_g, 256)
```

```python
import functools

import jax
import jax.numpy as jnp
from jax import lax
from jax.experimental import pallas as pl
from jax.experimental.pallas import tpu as pltpu

F32 = jnp.float32
BF16 = jnp.bfloat16

LANE = 128
EPS = 1e-6
NEG = -1e30
ROPE_BASE = 10000.0
GRID_W = 64
LOG2E = 1.4426950408889634
WINDOW = 128

MLA_HEADS = 8
MLA_NOPE = 128
MLA_ROPE = 64
MLA_V = 128
MLA_QK = MLA_NOPE + MLA_ROPE
MLA_QK_PAD = 2 * LANE
MLA_RANK = 512
SWA_HEADS = 16
SWA_KV_HEADS = 4
SWA_GROUP = SWA_HEADS // SWA_KV_HEADS
SWA_DIM = 64

VMEM_LIMIT = 56 * 1024 * 1024


def _params(*sem):
    return pltpu.CompilerParams(dimension_semantics=sem, vmem_limit_bytes=VMEM_LIMIT)


def _sigmoid(x):
    return 1.0 / (1.0 + jnp.exp(-x))


def _dot(a, b):
    return jnp.dot(a, b, preferred_element_type=F32)


def _dot_nt(a, b):
    return lax.dot_general(a, b, (((1,), (1,)), ((), ())), preferred_element_type=F32)


def _rope_tile(x, cos, sin_signed):
    lane = lax.broadcasted_iota(jnp.int32, x.shape, 1)
    first = (lane % 32) < 16
    partner = jnp.where(first, pltpu.roll(x, LANE - 16, 1), pltpu.roll(x, 16, 1))
    return x * cos + partner * sin_signed


def _split_bf16(a):
    hi = a.astype(BF16)
    lo = (a - hi.astype(F32)).astype(BF16)
    return hi, lo


def _mod_kernel(c_ref, w_ref, b_ref, o_ref):
    c = c_ref[...]
    s = c * _sigmoid(c)
    s_hi, s_lo = _split_bf16(s)
    w_hi, w_lo = _split_bf16(w_ref[...])
    acc = _dot(s_hi, w_hi) + (_dot(s_hi, w_lo) + _dot(s_lo, w_hi))
    o_ref[...] = acc + b_ref[...]


def _modulation(c_rows, w_ada, b_ada):
    rows, d = c_rows.shape
    n = w_ada.shape[1]
    tn = 1536
    return pl.pallas_call(
        _mod_kernel,
        grid=(n // tn,),
        in_specs=[pl.BlockSpec((rows, d), lambda j: (0, 0)),
                  pl.BlockSpec((d, tn), lambda j: (0, j)),
                  pl.BlockSpec((1, tn), lambda j: (0, j))],
        out_specs=pl.BlockSpec((rows, tn), lambda j: (0, j)),
        out_shape=jax.ShapeDtypeStruct((rows, n), F32),
        compiler_params=_params("arbitrary"),
        name="adaln_mod",
    )(c_rows, w_ada, b_ada.reshape(1, n))


def _norm_mod_kernel(x_ref, g_ref, shift_ref, scale_ref, h_ref):
    x = x_ref[0]
    y = x * lax.rsqrt(jnp.mean(x * x, axis=-1, keepdims=True) + EPS)
    h = (y * g_ref[...]) * (1.0 + scale_ref[0]) + shift_ref[0]
    h_ref[0] = h.astype(h_ref.dtype)


def _norm_mod(x, g, mod, tm):
    b, s, d = x.shape
    return pl.pallas_call(
        _norm_mod_kernel,
        grid=(b, s // tm),
        in_specs=[pl.BlockSpec((1, tm, d), lambda bi, i: (bi, i, 0)),
                  pl.BlockSpec((1, d), lambda bi, i: (0, 0)),
                  pl.BlockSpec((1, 1, d), lambda bi, i: (bi, 0, 0)),
                  pl.BlockSpec((1, 1, d), lambda bi, i: (bi, 0, 1))],
        out_specs=pl.BlockSpec((1, tm, d), lambda bi, i: (bi, i, 0)),
        out_shape=jax.ShapeDtypeStruct((b, s, d), BF16),
        compiler_params=_params("arbitrary", "arbitrary"),
        name="norm_mod",
    )(x, g.reshape(1, d), mod, mod)


def _gates_kernel(h_ref, w_ref, b_ref, o_ref, *, silu_blocks):
    z = _dot(h_ref[0], w_ref[...]) + b_ref[...]
    sg = _sigmoid(z)
    is_silu = pl.program_id(2) < silu_blocks
    o_ref[0] = (sg * jnp.where(is_silu, z, 1.0)).astype(o_ref.dtype)


def _gates(h, w, bias, silu_cols, tm, tn):
    b, s, d = h.shape
    n = w.shape[1]
    return pl.pallas_call(
        functools.partial(_gates_kernel, silu_blocks=silu_cols // tn),
        grid=(b, s // tm, n // tn),
        in_specs=[pl.BlockSpec((1, tm, d), lambda bi, i, j: (bi, i, 0)),
                  pl.BlockSpec((d, tn), lambda bi, i, j: (0, j)),
                  pl.BlockSpec((1, tn), lambda bi, i, j: (0, j))],
        out_specs=pl.BlockSpec((1, tm, tn), lambda bi, i, j: (bi, i, j)),
        out_shape=jax.ShapeDtypeStruct((b, s, n), BF16),
        compiler_params=_params("arbitrary", "arbitrary", "arbitrary"),
        name="gate_proj",
    )(h, w, bias.reshape(1, n))


def _rms(x, g):
    return (x * lax.rsqrt(jnp.mean(x * x, axis=-1, keepdims=True) + EPS)) * g


def _mla_proj_kernel(h_ref, w_ref, b_ref, qg_ref, kvg_ref, wuq_ref, wukv_ref, cos_ref, sin_ref,
                     q_ref, k_ref, vt_ref, *, q_scale):
    z = _dot(h_ref[0], w_ref[...]) + b_ref[...]
    cos = cos_ref[...]
    sin = sin_ref[...]
    cq = _rms(z[:, :MLA_RANK], qg_ref[...]).astype(BF16)
    ckv = _rms(z[:, MLA_RANK:2 * MLA_RANK], kvg_ref[...]).astype(BF16)
    k_pe = _rope_tile(z[:, 2 * MLA_RANK:], cos, sin).astype(k_ref.dtype)
    q = _dot(cq, wuq_ref[...])
    kv = _dot(ckv, wukv_ref[...])
    for hd in range(MLA_HEADS):
        base = hd * MLA_QK_PAD
        q_ref[0, :, base:base + LANE] = (q[:, base:base + LANE] * q_scale).astype(q_ref.dtype)
        q_rope = _rope_tile(q[:, base + LANE:base + 2 * LANE], cos, sin)
        q_ref[0, :, base + LANE:base + 2 * LANE] = (q_rope * q_scale).astype(q_ref.dtype)
        k_ref[0, :, base:base + LANE] = kv[:, base:base + LANE].astype(k_ref.dtype)
        k_ref[0, :, base + LANE:base + 2 * LANE] = k_pe
        vt_ref[0, 0, hd * MLA_V:(hd + 1) * MLA_V, :] = kv[:, base + LANE:base + 2 * LANE].T.astype(vt_ref.dtype)


def _mla_proj(h, w, bias, qg, kvg, wuq, wukv, cos, sin, tm, tkc):
    b, s, d = h.shape
    n = w.shape[1]
    nq = MLA_HEADS * MLA_QK_PAD
    nv = MLA_HEADS * MLA_V
    per_chunk = tkc // tm
    const = lambda bi, i: (0, 0)
    return pl.pallas_call(
        functools.partial(_mla_proj_kernel, q_scale=float(MLA_QK) ** -0.5 * LOG2E),
        grid=(b, s // tm),
        in_specs=[pl.BlockSpec((1, tm, d), lambda bi, i: (bi, i, 0)),
                  pl.BlockSpec((d, n), const),
                  pl.BlockSpec((1, n), const),
                  pl.BlockSpec((1, MLA_RANK), const),
                  pl.BlockSpec((1, MLA_RANK), const),
                  pl.BlockSpec((MLA_RANK, nq), const),
                  pl.BlockSpec((MLA_RANK, nq), const),
                  pl.BlockSpec((tm, LANE), lambda bi, i: (i, 0)),
                  pl.BlockSpec((tm, LANE), lambda bi, i: (i, 0))],
        out_specs=[pl.BlockSpec((1, tm, nq), lambda bi, i: (bi, i, 0)),
                   pl.BlockSpec((1, tm, nq), lambda bi, i: (bi, i, 0)),
                   pl.BlockSpec((1, 1, nv, tm), lambda bi, i: (bi, i // per_chunk, 0, i % per_chunk))],
        out_shape=[jax.ShapeDtypeStruct((b, s, nq), BF16),
                   jax.ShapeDtypeStruct((b, s, nq), BF16),
                   jax.ShapeDtypeStruct((b, s // tkc, nv, tkc), BF16)],
        compiler_params=_params("arbitrary", "arbitrary"),
        name="mla_proj",
    )(h, w, bias.reshape(1, n), qg.reshape(1, -1), kvg.reshape(1, -1), wuq, wukv, cos, sin)


def _swa_proj_kernel(h_ref, w_ref, b_ref, cos_ref, sin_ref, q_ref, k_ref, v_ref, *, nq, nk, q_scale):
    z = _dot(h_ref[0], w_ref[...]) + b_ref[...]
    cos = cos_ref[...]
    sin = sin_ref[...]
    for t in range(nq // LANE):
        q_t = _rope_tile(z[:, t * LANE:(t + 1) * LANE], cos, sin) * q_scale
        q_ref[0, :, t * LANE:(t + 1) * LANE] = q_t.astype(q_ref.dtype)
    for t in range(nk // LANE):
        k_t = _rope_tile(z[:, nq + t * LANE:nq + (t + 1) * LANE], cos, sin)
        k_ref[0, :, t * LANE:(t + 1) * LANE] = k_t.astype(k_ref.dtype)
    v_ref[0] = z[:, nq + nk:].astype(v_ref.dtype)


def _swa_proj(h, w, bias, cos, sin, tm):
    b, s, d = h.shape
    n = w.shape[1]
    nq = SWA_HEADS * SWA_DIM
    nk = SWA_KV_HEADS * LANE
    const = lambda bi, i: (0, 0)
    return pl.pallas_call(
        functools.partial(_swa_proj_kernel, nq=nq, nk=nk, q_scale=float(SWA_DIM) ** -0.5),
        grid=(b, s // tm),
        in_specs=[pl.BlockSpec((1, tm, d), lambda bi, i: (bi, i, 0)),
                  pl.BlockSpec((d, n), const),
                  pl.BlockSpec((1, n), const),
                  pl.BlockSpec((tm, LANE), lambda bi, i: (i, 0)),
                  pl.BlockSpec((tm, LANE), lambda bi, i: (i, 0))],
        out_specs=[pl.BlockSpec((1, tm, nq), lambda bi, i: (bi, i, 0)),
                   pl.BlockSpec((1, tm, nk), lambda bi, i: (bi, i, 0)),
                   pl.BlockSpec((1, tm, nk), lambda bi, i: (bi, i, 0))],
        out_shape=[jax.ShapeDtypeStruct((b, s, nq), BF16),
                   jax.ShapeDtypeStruct((b, s, nk), BF16),
                   jax.ShapeDtypeStruct((b, s, nk), BF16)],
        compiler_params=_params("arbitrary", "arbitrary"),
        name="swa_proj",
    )(h, w, bias.reshape(1, n), cos, sin)


def _mla_attn_kernel(q_ref, k_ref, vt_ref, kc_ref, vtc_ref, o_ref, s_a, s_b, acc_ref, *, tk, sub):
    tq = q_ref.shape[1]
    n_lat = k_ref.shape[1] // tk
    c = kc_ref.shape[1]
    s_bufs = (s_a, s_b)

    def fused_step(m, l, score=None, soft=None):
        if soft is not None:
            m_new = jnp.maximum(m, soft[3])
            alpha = jnp.exp2(m - m_new)
        cmax = lsum = pv = None
        keys = max(job[2] for job in (score, soft) if job is not None)
        for lo in range(0, keys, sub):
            piece = lambda job: slice(lo, min(lo + sub, job[2]))
            if score is not None and lo < score[2]:
                rows = piece(score)
                s = _dot_nt(score[0](rows), q_ref[0])
                score[1][rows, :] = s
                piece_max = jnp.max(s, axis=0, keepdims=True)
                cmax = piece_max if cmax is None else jnp.maximum(cmax, piece_max)
            if soft is not None and lo < soft[2]:
                rows = piece(soft)
                p = jnp.exp2(soft[0][rows, :] - m_new)
                piece_sum = jnp.sum(p, axis=0, keepdims=True)
                lsum = piece_sum if lsum is None else lsum + piece_sum
                d = _dot(soft[1](rows), p.astype(BF16))
                pv = d if pv is None else pv + d
        if soft is not None:
            acc_ref[...] = alpha * acc_ref[...] + pv
            return m_new, alpha * l + lsum, cmax
        return m, l, cmax

    def k_piece(n):
        return lambda rows: k_ref[0, pl.ds(pl.multiple_of(n * tk, tk) + rows.start, rows.stop - rows.start), :]

    def vt_piece(n):
        return lambda rows: vt_ref[0, n, :, rows]

    acc_ref[...] = jnp.zeros_like(acc_ref)
    m = jnp.full((1, tq), -jnp.inf, F32)
    l = jnp.zeros((1, tq), F32)
    _, _, cmax = fused_step(m, l, score=(k_piece(0), s_bufs[0], tk))

    def step(n, u, carry):
        m, l, cmax = carry
        return fused_step(m, l, score=(k_piece(n), s_bufs[u], tk),
                          soft=(s_bufs[1 - u], vt_piece(n - 1), tk, cmax))

    def pair(t, carry):
        return step(2 * t + 2, 0, step(2 * t + 1, 1, carry))

    m, l, cmax = lax.fori_loop(0, n_lat // 2 - 1, pair, (m, l, cmax))
    m, l, cmax = step(n_lat - 1, 1, (m, l, cmax))
    s_c = s_bufs[0].at[0:c]
    m, l, cmax_c = fused_step(m, l, score=(lambda rows: kc_ref[0, rows, :], s_c, c),
                              soft=(s_bufs[1], vt_piece(n_lat - 1), tk, cmax))
    m, l, _ = fused_step(m, l, soft=(s_c, lambda rows: vtc_ref[0, 0, :, rows], c, cmax_c))
    o_ref[0] = (acc_ref[...] / l).T.astype(o_ref.dtype)


def _mla_attn(q, k, vt, kc, vtc, tq, sub):
    b, s, _ = q.shape
    c = kc.shape[1]
    n_chunks, _, tk = vt.shape[1:]
    assert n_chunks % 2 == 0 and n_chunks >= 2 and c <= tk
    return pl.pallas_call(
        functools.partial(_mla_attn_kernel, tk=tk, sub=sub),
        grid=(b, MLA_HEADS, s // tq),
        in_specs=[pl.BlockSpec((1, tq, MLA_QK_PAD), lambda bi, hd, i: (bi, i, hd)),
                  pl.BlockSpec((1, s, MLA_QK_PAD), lambda bi, hd, i: (bi, 0, hd)),
                  pl.BlockSpec((1, n_chunks, MLA_V, tk), lambda bi, hd, i: (bi, 0, hd, 0)),
                  pl.BlockSpec((1, c, MLA_QK_PAD), lambda bi, hd, i: (bi, 0, hd)),
                  pl.BlockSpec((1, 1, MLA_V, c), lambda bi, hd, i: (bi, 0, hd, 0))],
        out_specs=pl.BlockSpec((1, tq, MLA_V), lambda bi, hd, i: (bi, i, hd)),
        out_shape=jax.ShapeDtypeStruct((b, s, MLA_HEADS * MLA_V), BF16),
        scratch_shapes=[pltpu.VMEM((tk, tq), F32), pltpu.VMEM((tk, tq), F32),
                        pltpu.VMEM((MLA_V, tq), F32)],
        compiler_params=_params("arbitrary", "arbitrary", "arbitrary"),
        name="mla_attn",
    )(q, k, vt, kc, vtc)


def _swa_attn_kernel(sink_ref, q_ref, k_ref, v_ref, kc_ref, vc_ref, o_ref, *, tq, span):
    kvh = pl.program_id(1)
    i = pl.program_id(2)
    s_len = k_ref.shape[1]
    q0 = i * tq
    start = pl.multiple_of(jnp.clip(q0 - WINDOW, 0, s_len - span), LANE)
    kw = k_ref[0, pl.ds(start, span), :]
    vw = v_ref[0, pl.ds(start, span), :]
    kc = kc_ref[0]
    vc = vc_ref[0]
    qpos = q0 + lax.broadcasted_iota(jnp.int32, (tq, span), 0)
    kpos = start + lax.broadcasted_iota(jnp.int32, (tq, span), 1)
    band = jnp.abs(qpos - kpos) <= WINDOW
    low_half = lax.broadcasted_iota(jnp.int32, (tq, LANE), 1) < SWA_DIM
    outs = []
    for g in range(SWA_GROUP):
        q_t = q_ref[0, :, (g // 2) * LANE:(g // 2 + 1) * LANE]
        keep = low_half if g % 2 == 0 else jnp.logical_not(low_half)
        q_h = jnp.where(keep, q_t, jnp.zeros_like(q_t))
        s_w = jnp.where(band, _dot_nt(q_h, kw), NEG)
        s_c = _dot_nt(q_h, kc)
        sink = sink_ref[kvh * SWA_GROUP + g]
        m = jnp.maximum(jnp.maximum(jnp.max(s_w, axis=1, keepdims=True),
                                    jnp.max(s_c, axis=1, keepdims=True)), sink)
        p_w = jnp.exp(s_w - m)
        p_c = jnp.exp(s_c - m)
        l = jnp.sum(p_w, axis=1, keepdims=True) + jnp.sum(p_c, axis=1, keepdims=True) + jnp.exp(sink - m)
        o = _dot(p_w.astype(vw.dtype), vw) + _dot(p_c.astype(vc.dtype), vc)
        outs.append(o / l)
    for t in range(SWA_GROUP // 2):
        o_ref[0, :, t * LANE:(t + 1) * LANE] = jnp.where(low_half, outs[2 * t], outs[2 * t + 1]).astype(o_ref.dtype)


def _swa_attn(sink, q, k, v, kc, vc, tq):
    b, s, _ = q.shape
    c = kc.shape[1]
    span = tq + 2 * WINDOW
    gw = SWA_GROUP * SWA_DIM
    return pl.pallas_call(
        functools.partial(_swa_attn_kernel, tq=tq, span=span),
        grid=(b, SWA_KV_HEADS, s // tq),
        in_specs=[pl.BlockSpec(memory_space=pltpu.SMEM),
                  pl.BlockSpec((1, tq, gw), lambda bi, kh, i: (bi, i, kh)),
                  pl.BlockSpec((1, s, LANE), lambda bi, kh, i: (bi, 0, kh)),
                  pl.BlockSpec((1, s, LANE), lambda bi, kh, i: (bi, 0, kh)),
                  pl.BlockSpec((1, c, LANE), lambda bi, kh, i: (bi, 0, kh)),
                  pl.BlockSpec((1, c, LANE), lambda bi, kh, i: (bi, 0, kh))],
        out_specs=pl.BlockSpec((1, tq, gw), lambda bi, kh, i: (bi, i, kh)),
        out_shape=jax.ShapeDtypeStruct((b, s, SWA_HEADS * SWA_DIM), BF16),
        compiler_params=_params("arbitrary", "arbitrary", "arbitrary"),
        name="swa_attn",
    )(sink, q, k, v, kc, vc)


def _merge_kernel(a_ref, b_ref, ga_ref, gb_ref, mga_ref, mgb_ref, x_ref, gate_ref,
                  wba_ref, wbb_ref, wo_ref, fg_ref, o_ref):
    a_in = (a_ref[0].astype(F32) * ga_ref[0].astype(F32)).astype(BF16)
    b_in = (b_ref[0].astype(F32) * gb_ref[0].astype(F32)).astype(BF16)
    y_a = _dot(a_in, wba_ref[...])
    y_b = _dot(b_in, wbb_ref[...])
    mix = mga_ref[0].astype(F32) * y_a + mgb_ref[0].astype(F32) * y_b
    out = _dot(mix.astype(BF16), wo_ref[...])
    xn = x_ref[0] + gate_ref[0] * out
    o_ref[0] = _rms(xn, fg_ref[...]).astype(o_ref.dtype)


def _merge(a, bh, gates, x, mod, wba, wbb, wo, fg, tm):
    b, s, d = x.shape
    wa = a.shape[2]
    const = lambda bi, i: (0, 0)
    row = lambda col: (lambda bi, i: (bi, i, col))
    return pl.pallas_call(
        _merge_kernel,
        grid=(b, s // tm),
        in_specs=[pl.BlockSpec((1, tm, wa), row(0)),
                  pl.BlockSpec((1, tm, wa), row(0)),
                  pl.BlockSpec((1, tm, wa), row(0)),
                  pl.BlockSpec((1, tm, wa), row(1)),
                  pl.BlockSpec((1, tm, d), row(1)),
                  pl.BlockSpec((1, tm, d), row(2)),
                  pl.BlockSpec((1, tm, d), row(0)),
                  pl.BlockSpec((1, 1, d), lambda bi, i: (bi, 0, 2)),
                  pl.BlockSpec((wa, d), const),
                  pl.BlockSpec((wa, d), const),
                  pl.BlockSpec((d, d), const),
                  pl.BlockSpec((1, d), const)],
        out_specs=pl.BlockSpec((1, tm, d), row(0)),
        out_shape=jax.ShapeDtypeStruct((b, s, d), x.dtype),
        compiler_params=_params("arbitrary", "arbitrary"),
        name="merge_out",
    )(a, bh, gates, gates, gates, gates, x, mod, wba, wbb, wo, fg.reshape(1, d))


def _rope_tables(s):
    t = jnp.arange(s, dtype=jnp.int32)
    n = 16
    inv = ROPE_BASE ** (-jnp.arange(n, dtype=F32) / n)
    ang_r = (t // GRID_W).astype(F32)[:, None] * inv[None, :]
    ang_c = (t % GRID_W).astype(F32)[:, None] * inv[None, :]
    cos = jnp.concatenate([jnp.cos(ang_r)] * 2 + [jnp.cos(ang_c)] * 2, axis=1)
    sin = jnp.concatenate([-jnp.sin(ang_r), jnp.sin(ang_r), -jnp.sin(ang_c), jnp.sin(ang_c)], axis=1)
    return jnp.tile(cos, (1, 2)), jnp.tile(sin, (1, 2))


def _dup_heads(w, heads, dim):
    lead = w.shape[:-1]
    w = w.reshape(lead + (heads, 1, dim))
    return jnp.broadcast_to(w, lead + (heads, 2, dim)).reshape(lead + (heads * 2 * dim,))


def kernel(x, c, ctx, c_ctx, w_ada, b_ada, norm_g, w_in, b_in, q_norm_g, kv_norm_g, w_uq, w_ukv, sink,
           w_branch_a, w_branch_b, w_out, final_g):
    bsz, s, d = x.shape
    c_len = ctx.shape[1]
    assert w_ada.shape[0] == 1, "single-layer block"
    widths = (MLA_RANK, MLA_RANK, MLA_ROPE, MLA_HEADS * MLA_V, SWA_HEADS * SWA_DIM,
              SWA_KV_HEADS * SWA_DIM, SWA_KV_HEADS * SWA_DIM, SWA_HEADS * SWA_DIM, d, d)
    assert sum(widths) == w_in.shape[2]
    offs = [0]
    for w_ in widths:
        offs.append(offs[-1] + w_)
    w_in0, b_in0 = w_in[0], b_in[0]
    col = lambda k: (w_in0[:, offs[k]:offs[k + 1]], b_in0[offs[k]:offs[k + 1]])
    (w_cq, b_cq), (w_ckv, b_ckv), (w_kpe, b_kpe), (w_ga, b_ga), (w_qs, b_qs) = (col(k) for k in range(5))
    (w_ks, b_ks), (w_vs, b_vs), (w_gb, b_gb), (w_mga, b_mga), (w_mgb, b_mgb) = (col(k) for k in range(5, 10))

    rows = 8
    c_rows = jnp.concatenate([c, c_ctx[None, :], jnp.zeros((rows - bsz - 1, d), F32)], axis=0)
    mod = _modulation(c_rows, w_ada[0], b_ada[0])
    mod_lat = mod[:bsz].reshape(bsz, 1, 3 * d)
    mod_ctx = jnp.broadcast_to(mod[bsz].reshape(1, 1, 3 * d), (bsz, 1, 3 * d))

    h = _norm_mod(x, norm_g[0], mod_lat, 512)
    h_c = _norm_mod(ctx, norm_g[0], mod_ctx, c_len)

    pad_pe = LANE - MLA_ROPE
    w_mla = jnp.concatenate([w_cq, w_ckv, w_kpe, jnp.zeros((d, pad_pe), F32)], axis=1).astype(BF16)
    b_mla = jnp.concatenate([b_cq, b_ckv, b_kpe, jnp.zeros((pad_pe,), F32)])
    wuq = jnp.pad(w_uq[0].reshape(MLA_RANK, MLA_HEADS, MLA_QK),
                  ((0, 0), (0, 0), (0, MLA_QK_PAD - MLA_QK))).reshape(MLA_RANK, -1).astype(BF16)
    wukv = w_ukv[0].astype(BF16)
    w_swa = jnp.concatenate([w_qs, _dup_heads(w_ks, SWA_KV_HEADS, SWA_DIM),
                             _dup_heads(w_vs, SWA_KV_HEADS, SWA_DIM)], axis=1).astype(BF16)
    b_swa = jnp.concatenate([b_qs, _dup_heads(b_ks, SWA_KV_HEADS, SWA_DIM),
                             _dup_heads(b_vs, SWA_KV_HEADS, SWA_DIM)])
    w_gates = jnp.concatenate([w_ga, w_gb, w_mga, w_mgb], axis=1).astype(BF16)
    b_gates = jnp.concatenate([b_ga, b_gb, b_mga, b_mgb])

    cos, sin = _rope_tables(s)
    cos_c = jnp.ones((c_len, LANE), F32)
    sin_c = jnp.zeros((c_len, LANE), F32)

    q_mla, k_mla, vt_mla = _mla_proj(h, w_mla, b_mla, q_norm_g[0], kv_norm_g[0], wuq, wukv, cos, sin, 512, 1024)
    _, kc_mla, vtc_mla = _mla_proj(h_c, w_mla, b_mla, q_norm_g[0], kv_norm_g[0], wuq, wukv, cos_c, sin_c,
                                   c_len, c_len)
    q_s, k_s, v_s = _swa_proj(h, w_swa, b_swa, cos, sin, 512)
    _, kc_s, vc_s = _swa_proj(h_c, w_swa, b_swa, cos_c, sin_c, c_len)
    gates = _gates(h, w_gates, b_gates, 2 * MLA_HEADS * MLA_V, 1024, 1024)

    a_heads = _mla_attn(q_mla, k_mla, vt_mla, kc_mla, vtc_mla, 512, 512)
    b_heads = _swa_attn(sink[0], q_s, k_s, v_s, kc_s, vc_s, 256)

    return _merge(a_heads, b_heads, gates, x, mod_lat, w_branch_a[0].astype(BF16),
                  w_branch_b[0].astype(BF16), w_out[0].astype(BF16), final_g, 256)
```

```python
import functools

import jax
import jax.numpy as jnp
from jax import lax
from jax.experimental import pallas as pl
from jax.experimental.pallas import tpu as pltpu

F32 = jnp.float32
BF16 = jnp.bfloat16

LANE = 128
EPS = 1e-6
NEG = -1e30
ROPE_BASE = 10000.0
GRID_W = 64
LOG2E = 1.4426950408889634
WINDOW = 128

MLA_HEADS = 8
MLA_NOPE = 128
MLA_ROPE = 64
MLA_V = 128
MLA_QK = MLA_NOPE + MLA_ROPE
MLA_QK_PAD = 2 * LANE
MLA_RANK = 512
SWA_HEADS = 16
SWA_KV_HEADS = 4
SWA_GROUP = SWA_HEADS // SWA_KV_HEADS
SWA_DIM = 64

VMEM_LIMIT = 56 * 1024 * 1024

NORM_TM = 512
PROJ_TM = 512
GATE_TM, GATE_TN = 1024, 1024
MLA_TQ = 1024
MLA_TK = 2048
MLA_SUB = 512
SWA_TQ = 256
MERGE_TM = 256


def _params(*sem):
    return pltpu.CompilerParams(dimension_semantics=sem, vmem_limit_bytes=VMEM_LIMIT)


def _sigmoid(x):
    return 1.0 / (1.0 + jnp.exp(-x))


def _dot(a, b):
    return jnp.dot(a, b, preferred_element_type=F32)


def _dot_nt(a, b):
    return lax.dot_general(a, b, (((1,), (1,)), ((), ())), preferred_element_type=F32)


def _rope_tile(x, cos, sin_signed):
    lane = lax.broadcasted_iota(jnp.int32, x.shape, 1)
    first = (lane % 32) < 16
    partner = jnp.where(first, pltpu.roll(x, LANE - 16, 1), pltpu.roll(x, 16, 1))
    return x * cos + partner * sin_signed


def _split_bf16(a):
    hi = a.astype(BF16)
    lo = (a - hi.astype(F32)).astype(BF16)
    return hi, lo


def _mod_kernel(c_ref, w_ref, b_ref, o_ref):
    c = c_ref[...]
    s = c * _sigmoid(c)
    s_hi, s_lo = _split_bf16(s)
    w_hi, w_lo = _split_bf16(w_ref[...])
    acc = _dot(s_hi, w_hi) + (_dot(s_hi, w_lo) + _dot(s_lo, w_hi))
    o_ref[...] = acc + b_ref[...]


def _modulation(c_rows, w_ada, b_ada):
    rows, d = c_rows.shape
    n = w_ada.shape[1]
    tn = 1536
    return pl.pallas_call(
        _mod_kernel,
        grid=(n // tn,),
        in_specs=[pl.BlockSpec((rows, d), lambda j: (0, 0)),
                  pl.BlockSpec((d, tn), lambda j: (0, j)),
                  pl.BlockSpec((1, tn), lambda j: (0, j))],
        out_specs=pl.BlockSpec((rows, tn), lambda j: (0, j)),
        out_shape=jax.ShapeDtypeStruct((rows, n), F32),
        compiler_params=_params("arbitrary"),
        name="adaln_mod",
    )(c_rows, w_ada, b_ada.reshape(1, n))


def _norm_mod_kernel(x_ref, g_ref, shift_ref, scale_ref, h_ref):
    x = x_ref[0]
    y = x * lax.rsqrt(jnp.mean(x * x, axis=-1, keepdims=True) + EPS)
    h = (y * g_ref[...]) * (1.0 + scale_ref[0]) + shift_ref[0]
    h_ref[0] = h.astype(h_ref.dtype)


def _norm_mod(x, g, mod, tm):
    b, s, d = x.shape
    return pl.pallas_call(
        _norm_mod_kernel,
        grid=(b, s // tm),
        in_specs=[pl.BlockSpec((1, tm, d), lambda bi, i: (bi, i, 0)),
                  pl.BlockSpec((1, d), lambda bi, i: (0, 0)),
                  pl.BlockSpec((1, 1, d), lambda bi, i: (bi, 0, 0)),
                  pl.BlockSpec((1, 1, d), lambda bi, i: (bi, 0, 1))],
        out_specs=pl.BlockSpec((1, tm, d), lambda bi, i: (bi, i, 0)),
        out_shape=jax.ShapeDtypeStruct((b, s, d), BF16),
        compiler_params=_params("arbitrary", "arbitrary"),
        name="norm_mod",
    )(x, g.reshape(1, d), mod, mod)


def _gates_kernel(h_ref, w_ref, b_ref, o_ref, *, silu_blocks):
    z = _dot(h_ref[0], w_ref[...]) + b_ref[...]
    sg = _sigmoid(z)
    is_silu = pl.program_id(2) < silu_blocks
    o_ref[0] = (sg * jnp.where(is_silu, z, 1.0)).astype(o_ref.dtype)


def _gates(h, w, bias, silu_cols, tm, tn):
    b, s, d = h.shape
    n = w.shape[1]
    return pl.pallas_call(
        functools.partial(_gates_kernel, silu_blocks=silu_cols // tn),
        grid=(b, s // tm, n // tn),
        in_specs=[pl.BlockSpec((1, tm, d), lambda bi, i, j: (bi, i, 0)),
                  pl.BlockSpec((d, tn), lambda bi, i, j: (0, j)),
                  pl.BlockSpec((1, tn), lambda bi, i, j: (0, j))],
        out_specs=pl.BlockSpec((1, tm, tn), lambda bi, i, j: (bi, i, j)),
        out_shape=jax.ShapeDtypeStruct((b, s, n), BF16),
        compiler_params=_params("arbitrary", "arbitrary", "arbitrary"),
        name="gate_proj",
    )(h, w, bias.reshape(1, n))


def _rms(x, g):
    return (x * lax.rsqrt(jnp.mean(x * x, axis=-1, keepdims=True) + EPS)) * g


def _mla_proj_kernel(h_ref, w_ref, b_ref, qg_ref, kvg_ref, wuq_ref, wukv_ref, cos_ref, sin_ref,
                     q_ref, k_ref, vt_ref, *, q_scale):
    z = _dot(h_ref[0], w_ref[...]) + b_ref[...]
    cos = cos_ref[...]
    sin = sin_ref[...]
    cq = _rms(z[:, :MLA_RANK], qg_ref[...]).astype(BF16)
    ckv = _rms(z[:, MLA_RANK:2 * MLA_RANK], kvg_ref[...]).astype(BF16)
    k_pe = _rope_tile(z[:, 2 * MLA_RANK:], cos, sin).astype(k_ref.dtype)
    q = _dot(cq, wuq_ref[...])
    kv = _dot(ckv, wukv_ref[...])
    for hd in range(MLA_HEADS):
        base = hd * MLA_QK_PAD
        q_ref[0, :, base:base + LANE] = (q[:, base:base + LANE] * q_scale).astype(q_ref.dtype)
        q_rope = _rope_tile(q[:, base + LANE:base + 2 * LANE], cos, sin)
        q_ref[0, :, base + LANE:base + 2 * LANE] = (q_rope * q_scale).astype(q_ref.dtype)
        k_ref[0, :, base:base + LANE] = kv[:, base:base + LANE].astype(k_ref.dtype)
        k_ref[0, :, base + LANE:base + 2 * LANE] = k_pe
        vt_ref[0, 0, hd * MLA_V:(hd + 1) * MLA_V, :] = kv[:, base + LANE:base + 2 * LANE].T.astype(vt_ref.dtype)


def _mla_proj(h, w, bias, qg, kvg, wuq, wukv, cos, sin, tm, tkc):
    b, s, d = h.shape
    n = w.shape[1]
    nq = MLA_HEADS * MLA_QK_PAD
    nv = MLA_HEADS * MLA_V
    per_chunk = tkc // tm
    const = lambda bi, i: (0, 0)
    return pl.pallas_call(
        functools.partial(_mla_proj_kernel, q_scale=float(MLA_QK) ** -0.5 * LOG2E),
        grid=(b, s // tm),
        in_specs=[pl.BlockSpec((1, tm, d), lambda bi, i: (bi, i, 0)),
                  pl.BlockSpec((d, n), const),
                  pl.BlockSpec((1, n), const),
                  pl.BlockSpec((1, MLA_RANK), const),
                  pl.BlockSpec((1, MLA_RANK), const),
                  pl.BlockSpec((MLA_RANK, nq), const),
                  pl.BlockSpec((MLA_RANK, nq), const),
                  pl.BlockSpec((tm, LANE), lambda bi, i: (i, 0)),
                  pl.BlockSpec((tm, LANE), lambda bi, i: (i, 0))],
        out_specs=[pl.BlockSpec((1, tm, nq), lambda bi, i: (bi, i, 0)),
                   pl.BlockSpec((1, tm, nq), lambda bi, i: (bi, i, 0)),
                   pl.BlockSpec((1, 1, nv, tm), lambda bi, i: (bi, i // per_chunk, 0, i % per_chunk))],
        out_shape=[jax.ShapeDtypeStruct((b, s, nq), BF16),
                   jax.ShapeDtypeStruct((b, s, nq), BF16),
                   jax.ShapeDtypeStruct((b, s // tkc, nv, tkc), BF16)],
        compiler_params=_params("arbitrary", "arbitrary"),
        name="mla_proj",
    )(h, w, bias.reshape(1, n), qg.reshape(1, -1), kvg.reshape(1, -1), wuq, wukv, cos, sin)


def _swa_proj_kernel(h_ref, w_ref, b_ref, cos_ref, sin_ref, q_ref, k_ref, v_ref, *, nq, nk, q_scale):
    z = _dot(h_ref[0], w_ref[...]) + b_ref[...]
    cos = cos_ref[...]
    sin = sin_ref[...]
    for t in range(nq // LANE):
        q_t = _rope_tile(z[:, t * LANE:(t + 1) * LANE], cos, sin) * q_scale
        q_ref[0, :, t * LANE:(t + 1) * LANE] = q_t.astype(q_ref.dtype)
    for t in range(nk // LANE):
        k_t = _rope_tile(z[:, nq + t * LANE:nq + (t + 1) * LANE], cos, sin)
        k_ref[0, :, t * LANE:(t + 1) * LANE] = k_t.astype(k_ref.dtype)
    v_ref[0] = z[:, nq + nk:].astype(v_ref.dtype)


def _swa_proj(h, w, bias, cos, sin, tm):
    b, s, d = h.shape
    n = w.shape[1]
    nq = SWA_HEADS * SWA_DIM
    nk = SWA_KV_HEADS * LANE
    const = lambda bi, i: (0, 0)
    return pl.pallas_call(
        functools.partial(_swa_proj_kernel, nq=nq, nk=nk, q_scale=float(SWA_DIM) ** -0.5),
        grid=(b, s // tm),
        in_specs=[pl.BlockSpec((1, tm, d), lambda bi, i: (bi, i, 0)),
                  pl.BlockSpec((d, n), const),
                  pl.BlockSpec((1, n), const),
                  pl.BlockSpec((tm, LANE), lambda bi, i: (i, 0)),
                  pl.BlockSpec((tm, LANE), lambda bi, i: (i, 0))],
        out_specs=[pl.BlockSpec((1, tm, nq), lambda bi, i: (bi, i, 0)),
                   pl.BlockSpec((1, tm, nk), lambda bi, i: (bi, i, 0)),
                   pl.BlockSpec((1, tm, nk), lambda bi, i: (bi, i, 0))],
        out_shape=[jax.ShapeDtypeStruct((b, s, nq), BF16),
                   jax.ShapeDtypeStruct((b, s, nk), BF16),
                   jax.ShapeDtypeStruct((b, s, nk), BF16)],
        compiler_params=_params("arbitrary", "arbitrary"),
        name="swa_proj",
    )(h, w, bias.reshape(1, n), cos, sin)


def _mla_attn_kernel(q_ref, qn_ref, k_ref, vt_ref, kc_ref, vtc_ref, o_ref, s_a, s_b, acc_ref, cmax_ref,
                     *, tk, sub):
    tq = q_ref.shape[1]
    n_lat = k_ref.shape[1] // tk
    c = kc_ref.shape[1]
    s_bufs = (s_a, s_b)

    def pieces(n, with_ctx=False):
        out = []
        for lo in range(0, tk, sub):
            hi = min(lo + sub, tk)
            out.append((slice(lo, hi),
                        lambda lo=lo, hi=hi: k_ref[0, pl.ds(pl.multiple_of(n * tk, tk) + lo, hi - lo), :],
                        lambda lo=lo, hi=hi: vt_ref[0, n, :, lo:hi]))
        if with_ctx:
            out.append((slice(tk, tk + c), lambda: kc_ref[0], lambda: vtc_ref[0, 0]))
        return out

    def fused_step(m, l, score=None, soft=None):
        if soft is not None:
            m_new = jnp.maximum(m, soft[2])
            alpha = jnp.exp2(m - m_new)
        cmax = lsum = pv = None
        n_score = len(score[0]) if score is not None else 0
        n_soft = len(soft[0]) if soft is not None else 0
        for j in range(max(n_score, n_soft)):
            if j < n_score:
                rows, k_load, _ = score[0][j]
                s = _dot_nt(k_load(), score[2][0])
                score[1][rows, :] = s
                piece_max = jnp.max(s, axis=0, keepdims=True)
                cmax = piece_max if cmax is None else jnp.maximum(cmax, piece_max)
            if j < n_soft:
                rows, _, vt_load = soft[0][j]
                p = jnp.exp2(soft[1][rows, :] - m_new)
                piece_sum = jnp.sum(p, axis=0, keepdims=True)
                lsum = piece_sum if lsum is None else lsum + piece_sum
                d = _dot(vt_load(), p.astype(BF16))
                pv = d if pv is None else pv + d
        if soft is not None:
            acc_ref[...] = alpha * acc_ref[...] + pv
            return m_new, alpha * l + lsum, cmax
        return m, l, cmax

    @pl.when(pl.program_id(2) == 0)
    def _():
        cmax_ref[...] = fused_step(None, None, score=(pieces(0), s_bufs[0], q_ref))[2]

    acc_ref[...] = jnp.zeros_like(acc_ref)
    m = jnp.full((1, tq), -jnp.inf, F32)
    l = jnp.zeros((1, tq), F32)
    cmax = cmax_ref[...]

    def step(n, u, carry, last=False):
        m, l, cmax = carry
        return fused_step(m, l, score=(pieces(n, last), s_bufs[u], q_ref),
                          soft=(pieces(n - 1), s_bufs[1 - u], cmax))

    def pair(t, carry):
        return step(2 * t + 2, 0, step(2 * t + 1, 1, carry))

    carry = lax.fori_loop(0, n_lat // 2 - 1, pair, (m, l, cmax))
    m, l, cmax = step(n_lat - 1, 1, carry, last=True)
    m, l, cmax_next = fused_step(m, l, score=(pieces(0), s_bufs[0], qn_ref),
                                 soft=(pieces(n_lat - 1, True), s_bufs[1], cmax))
    cmax_ref[...] = cmax_next
    o_ref[0] = (acc_ref[...] / l).T.astype(o_ref.dtype)


def _mla_attn(q, k, vt, kc, vtc, tq, sub):
    b, s, _ = q.shape
    c = kc.shape[1]
    n_chunks, _, tk = vt.shape[1:]
    n_q = s // tq
    assert n_chunks % 2 == 0 and n_chunks >= 2
    return pl.pallas_call(
        functools.partial(_mla_attn_kernel, tk=tk, sub=sub),
        grid=(b, MLA_HEADS, n_q),
        in_specs=[pl.BlockSpec((1, tq, MLA_QK_PAD), lambda bi, hd, i: (bi, i, hd)),
                  pl.BlockSpec((1, tq, MLA_QK_PAD), lambda bi, hd, i: (bi, jnp.minimum(i + 1, n_q - 1), hd)),
                  pl.BlockSpec((1, s, MLA_QK_PAD), lambda bi, hd, i: (bi, 0, hd)),
                  pl.BlockSpec((1, n_chunks, MLA_V, tk), lambda bi, hd, i: (bi, 0, hd, 0)),
                  pl.BlockSpec((1, c, MLA_QK_PAD), lambda bi, hd, i: (bi, 0, hd)),
                  pl.BlockSpec((1, 1, MLA_V, c), lambda bi, hd, i: (bi, 0, hd, 0))],
        out_specs=pl.BlockSpec((1, tq, MLA_V), lambda bi, hd, i: (bi, i, hd)),
        out_shape=jax.ShapeDtypeStruct((b, s, MLA_HEADS * MLA_V), BF16),
        scratch_shapes=[pltpu.VMEM((tk + c, tq), F32), pltpu.VMEM((tk + c, tq), F32),
                        pltpu.VMEM((MLA_V, tq), F32), pltpu.VMEM((1, tq), F32)],
        compiler_params=_params("arbitrary", "arbitrary", "arbitrary"),
        name="mla_attn",
    )(q, q, k, vt, kc, vtc)


def _swa_attn_kernel(sink_ref, q_ref, k_ref, v_ref, kc_ref, vc_ref, o_ref, *, tq, span):
    kvh = pl.program_id(1)
    i = pl.program_id(2)
    s_len = k_ref.shape[1]
    q0 = i * tq
    start = pl.multiple_of(jnp.clip(q0 - WINDOW, 0, s_len - span), LANE)
    kw = k_ref[0, pl.ds(start, span), :]
    vw = v_ref[0, pl.ds(start, span), :]
    kc = kc_ref[0]
    vc = vc_ref[0]
    qpos = q0 + lax.broadcasted_iota(jnp.int32, (tq, span), 0)
    kpos = start + lax.broadcasted_iota(jnp.int32, (tq, span), 1)
    band = jnp.abs(qpos - kpos) <= WINDOW
    low_half = lax.broadcasted_iota(jnp.int32, (tq, LANE), 1) < SWA_DIM
    outs = []
    for g in range(SWA_GROUP):
        q_t = q_ref[0, :, (g // 2) * LANE:(g // 2 + 1) * LANE]
        keep = low_half if g % 2 == 0 else jnp.logical_not(low_half)
        q_h = jnp.where(keep, q_t, jnp.zeros_like(q_t))
        s_w = jnp.where(band, _dot_nt(q_h, kw), NEG)
        s_c = _dot_nt(q_h, kc)
        sink = sink_ref[kvh * SWA_GROUP + g]
        m = jnp.maximum(jnp.maximum(jnp.max(s_w, axis=1, keepdims=True),
                                    jnp.max(s_c, axis=1, keepdims=True)), sink)
        p_w = jnp.exp(s_w - m)
        p_c = jnp.exp(s_c - m)
        l = jnp.sum(p_w, axis=1, keepdims=True) + jnp.sum(p_c, axis=1, keepdims=True) + jnp.exp(sink - m)
        o = _dot(p_w.astype(vw.dtype), vw) + _dot(p_c.astype(vc.dtype), vc)
        outs.append(o / l)
    for t in range(SWA_GROUP // 2):
        o_ref[0, :, t * LANE:(t + 1) * LANE] = jnp.where(low_half, outs[2 * t], outs[2 * t + 1]).astype(o_ref.dtype)


def _swa_attn(sink, q, k, v, kc, vc, tq):
    b, s, _ = q.shape
    c = kc.shape[1]
    span = tq + 2 * WINDOW
    gw = SWA_GROUP * SWA_DIM
    return pl.pallas_call(
        functools.partial(_swa_attn_kernel, tq=tq, span=span),
        grid=(b, SWA_KV_HEADS, s // tq),
        in_specs=[pl.BlockSpec(memory_space=pltpu.SMEM),
                  pl.BlockSpec((1, tq, gw), lambda bi, kh, i: (bi, i, kh)),
                  pl.BlockSpec((1, s, LANE), lambda bi, kh, i: (bi, 0, kh)),
                  pl.BlockSpec((1, s, LANE), lambda bi, kh, i: (bi, 0, kh)),
                  pl.BlockSpec((1, c, LANE), lambda bi, kh, i: (bi, 0, kh)),
                  pl.BlockSpec((1, c, LANE), lambda bi, kh, i: (bi, 0, kh))],
        out_specs=pl.BlockSpec((1, tq, gw), lambda bi, kh, i: (bi, i, kh)),
        out_shape=jax.ShapeDtypeStruct((b, s, SWA_HEADS * SWA_DIM), BF16),
        compiler_params=_params("arbitrary", "arbitrary", "arbitrary"),
        name="swa_attn",
    )(sink, q, k, v, kc, vc)


def _merge_kernel(a_ref, b_ref, ga_ref, gb_ref, mga_ref, mgb_ref, x_ref, gate_ref,
                  wba_ref, wbb_ref, wo_ref, fg_ref, o_ref):
    a_in = (a_ref[0].astype(F32) * ga_ref[0].astype(F32)).astype(BF16)
    b_in = (b_ref[0].astype(F32) * gb_ref[0].astype(F32)).astype(BF16)
    y_a = _dot(a_in, wba_ref[...])
    y_b = _dot(b_in, wbb_ref[...])
    mix = mga_ref[0].astype(F32) * y_a + mgb_ref[0].astype(F32) * y_b
    out = _dot(mix.astype(BF16), wo_ref[...])
    xn = x_ref[0] + gate_ref[0] * out
    o_ref[0] = _rms(xn, fg_ref[...]).astype(o_ref.dtype)


def _merge(a, bh, gates, x, mod, wba, wbb, wo, fg, tm):
    b, s, d = x.shape
    wa = a.shape[2]
    const = lambda bi, i: (0, 0)
    row = lambda col: (lambda bi, i: (bi, i, col))
    return pl.pallas_call(
        _merge_kernel,
        grid=(b, s // tm),
        in_specs=[pl.BlockSpec((1, tm, wa), row(0)),
                  pl.BlockSpec((1, tm, wa), row(0)),
                  pl.BlockSpec((1, tm, wa), row(0)),
                  pl.BlockSpec((1, tm, wa), row(1)),
                  pl.BlockSpec((1, tm, d), row(1)),
                  pl.BlockSpec((1, tm, d), row(2)),
                  pl.BlockSpec((1, tm, d), row(0)),
                  pl.BlockSpec((1, 1, d), lambda bi, i: (bi, 0, 2)),
                  pl.BlockSpec((wa, d), const),
                  pl.BlockSpec((wa, d), const),
                  pl.BlockSpec((d, d), const),
                  pl.BlockSpec((1, d), const)],
        out_specs=pl.BlockSpec((1, tm, d), row(0)),
        out_shape=jax.ShapeDtypeStruct((b, s, d), x.dtype),
        compiler_params=_params("arbitrary", "arbitrary"),
        name="merge_out",
    )(a, bh, gates, gates, gates, gates, x, mod, wba, wbb, wo, fg.reshape(1, d))


def _rope_tables(s):
    t = jnp.arange(s, dtype=jnp.int32)
    n = 16
    inv = ROPE_BASE ** (-jnp.arange(n, dtype=F32) / n)
    ang_r = (t // GRID_W).astype(F32)[:, None] * inv[None, :]
    ang_c = (t % GRID_W).astype(F32)[:, None] * inv[None, :]
    cos = jnp.concatenate([jnp.cos(ang_r)] * 2 + [jnp.cos(ang_c)] * 2, axis=1)
    sin = jnp.concatenate([-jnp.sin(ang_r), jnp.sin(ang_r), -jnp.sin(ang_c), jnp.sin(ang_c)], axis=1)
    return jnp.tile(cos, (1, 2)), jnp.tile(sin, (1, 2))


def _dup_heads(w, heads, dim):
    lead = w.shape[:-1]
    w = w.reshape(lead + (heads, 1, dim))
    return jnp.broadcast_to(w, lead + (heads, 2, dim)).reshape(lead + (heads * 2 * dim,))


def kernel(x, c, ctx, c_ctx, w_ada, b_ada, norm_g, w_in, b_in, q_norm_g, kv_norm_g, w_uq, w_ukv, sink,
           w_branch_a, w_branch_b, w_out, final_g):
    bsz, s, d = x.shape
    c_len = ctx.shape[1]
    assert w_ada.shape[0] == 1, "single-layer block"
    widths = (MLA_RANK, MLA_RANK, MLA_ROPE, MLA_HEADS * MLA_V, SWA_HEADS * SWA_DIM,
              SWA_KV_HEADS * SWA_DIM, SWA_KV_HEADS * SWA_DIM, SWA_HEADS * SWA_DIM, d, d)
    assert sum(widths) == w_in.shape[2]
    offs = [0]
    for w_ in widths:
        offs.append(offs[-1] + w_)
    w_in0, b_in0 = w_in[0], b_in[0]
    col = lambda k: (w_in0[:, offs[k]:offs[k + 1]], b_in0[offs[k]:offs[k + 1]])
    (w_cq, b_cq), (w_ckv, b_ckv), (w_kpe, b_kpe), (w_ga, b_ga), (w_qs, b_qs) = (col(k) for k in range(5))
    (w_ks, b_ks), (w_vs, b_vs), (w_gb, b_gb), (w_mga, b_mga), (w_mgb, b_mgb) = (col(k) for k in range(5, 10))

    rows = 8
    c_rows = jnp.concatenate([c, c_ctx[None, :], jnp.zeros((rows - bsz - 1, d), F32)], axis=0)
    mod = _modulation(c_rows, w_ada[0], b_ada[0])
    mod_lat = mod[:bsz].reshape(bsz, 1, 3 * d)
    mod_ctx = jnp.broadcast_to(mod[bsz].reshape(1, 1, 3 * d), (bsz, 1, 3 * d))

    h = _norm_mod(x, norm_g[0], mod_lat, NORM_TM)
    h_c = _norm_mod(ctx, norm_g[0], mod_ctx, c_len)

    pad_pe = LANE - MLA_ROPE
    w_mla = jnp.concatenate([w_cq, w_ckv, w_kpe, jnp.zeros((d, pad_pe), F32)], axis=1).astype(BF16)
    b_mla = jnp.concatenate([b_cq, b_ckv, b_kpe, jnp.zeros((pad_pe,), F32)])
    wuq = jnp.pad(w_uq[0].reshape(MLA_RANK, MLA_HEADS, MLA_QK),
                  ((0, 0), (0, 0), (0, MLA_QK_PAD - MLA_QK))).reshape(MLA_RANK, -1).astype(BF16)
    wukv = w_ukv[0].astype(BF16)
    w_swa = jnp.concatenate([w_qs, _dup_heads(w_ks, SWA_KV_HEADS, SWA_DIM),
                             _dup_heads(w_vs, SWA_KV_HEADS, SWA_DIM)], axis=1).astype(BF16)
    b_swa = jnp.concatenate([b_qs, _dup_heads(b_ks, SWA_KV_HEADS, SWA_DIM),
                             _dup_heads(b_vs, SWA_KV_HEADS, SWA_DIM)])
    w_gates = jnp.concatenate([w_ga, w_gb, w_mga, w_mgb], axis=1).astype(BF16)
    b_gates = jnp.concatenate([b_ga, b_gb, b_mga, b_mgb])

    cos, sin = _rope_tables(s)
    cos_c = jnp.ones((c_len, LANE), F32)
    sin_c = jnp.zeros((c_len, LANE), F32)

    q_mla, k_mla, vt_mla = _mla_proj(h, w_mla, b_mla, q_norm_g[0], kv_norm_g[0], wuq, wukv, cos, sin,
                                     PROJ_TM, MLA_TK)
    _, kc_mla, vtc_mla = _mla_proj(h_c, w_mla, b_mla, q_norm_g[0], kv_norm_g[0], wuq, wukv, cos_c, sin_c,
                                   c_len, c_len)
    q_s, k_s, v_s = _swa_proj(h, w_swa, b_swa, cos, sin, PROJ_TM)
    _, kc_s, vc_s = _swa_proj(h_c, w_swa, b_swa, cos_c, sin_c, c_len)
    gates = _gates(h, w_gates, b_gates, 2 * MLA_HEADS * MLA_V, GATE_TM, GATE_TN)

    a_heads = _mla_attn(q_mla, k_mla, vt_mla, kc_mla, vtc_mla, MLA_TQ, MLA_SUB)
    b_heads = _swa_attn(sink[0], q_s, k_s, v_s, kc_s, vc_s, SWA_TQ)

    return _merge(a_heads, b_heads, gates, x, mod_lat, w_branch_a[0].astype(BF16),
                  w_branch_b[0].astype(BF16), w_out[0].astype(BF16), final_g, MERGE_TM)
```

```python
import functools

import jax
import jax.numpy as jnp
from jax import lax
from jax.experimental import pallas as pl
from jax.experimental.pallas import tpu as pltpu

F32 = jnp.float32
BF16 = jnp.bfloat16

LANE = 128
EPS = 1e-6
NEG = -1e30
ROPE_BASE = 10000.0
GRID_W = 64
LOG2E = 1.4426950408889634
WINDOW = 128

MLA_HEADS = 8
MLA_NOPE = 128
MLA_ROPE = 64
MLA_V = 128
MLA_QK = MLA_NOPE + MLA_ROPE
MLA_QK_PAD = 2 * LANE
MLA_RANK = 512
SWA_HEADS = 16
SWA_KV_HEADS = 4
SWA_GROUP = SWA_HEADS // SWA_KV_HEADS
SWA_DIM = 64

VMEM_LIMIT = 56 * 1024 * 1024

NORM_TM = 512
PROJ_TM = 512
GATE_TM, GATE_TN = 1024, 2048
MLA_TQ = 1024
MLA_TK = 2048
MLA_SUB = 512
SWA_TQ = 512
MERGE_TM = 256


def _params(*sem):
    return pltpu.CompilerParams(dimension_semantics=sem, vmem_limit_bytes=VMEM_LIMIT)


def _sigmoid(x):
    return 1.0 / (1.0 + jnp.exp(-x))


def _dot(a, b):
    return jnp.dot(a, b, preferred_element_type=F32)


def _dot_nt(a, b):
    return lax.dot_general(a, b, (((1,), (1,)), ((), ())), preferred_element_type=F32)


def _rope_tile(x, cos, sin_signed):
    lane = lax.broadcasted_iota(jnp.int32, x.shape, 1)
    first = (lane % 32) < 16
    partner = jnp.where(first, pltpu.roll(x, LANE - 16, 1), pltpu.roll(x, 16, 1))
    return x * cos + partner * sin_signed


def _split_bf16(a):
    hi = a.astype(BF16)
    lo = (a - hi.astype(F32)).astype(BF16)
    return hi, lo


def _mod_kernel(c_ref, w_ref, b_ref, o_ref):
    c = c_ref[...]
    s = c * _sigmoid(c)
    s_hi, s_lo = _split_bf16(s)
    w_hi, w_lo = _split_bf16(w_ref[...])
    acc = _dot(s_hi, w_hi) + (_dot(s_hi, w_lo) + _dot(s_lo, w_hi))
    o_ref[...] = acc + b_ref[...]


def _modulation(c_rows, w_ada, b_ada):
    rows, d = c_rows.shape
    n = w_ada.shape[1]
    tn = 1536
    return pl.pallas_call(
        _mod_kernel,
        grid=(n // tn,),
        in_specs=[pl.BlockSpec((rows, d), lambda j: (0, 0)),
                  pl.BlockSpec((d, tn), lambda j: (0, j)),
                  pl.BlockSpec((1, tn), lambda j: (0, j))],
        out_specs=pl.BlockSpec((rows, tn), lambda j: (0, j)),
        out_shape=jax.ShapeDtypeStruct((rows, n), F32),
        compiler_params=_params("arbitrary"),
        name="adaln_mod",
    )(c_rows, w_ada, b_ada.reshape(1, n))


def _norm_mod_kernel(x_ref, g_ref, shift_ref, scale_ref, h_ref):
    x = x_ref[0]
    y = x * lax.rsqrt(jnp.mean(x * x, axis=-1, keepdims=True) + EPS)
    h = (y * g_ref[...]) * (1.0 + scale_ref[0]) + shift_ref[0]
    h_ref[0] = h.astype(h_ref.dtype)


def _norm_mod(x, g, mod, tm):
    b, s, d = x.shape
    return pl.pallas_call(
        _norm_mod_kernel,
        grid=(b, s // tm),
        in_specs=[pl.BlockSpec((1, tm, d), lambda bi, i: (bi, i, 0)),
                  pl.BlockSpec((1, d), lambda bi, i: (0, 0)),
                  pl.BlockSpec((1, 1, d), lambda bi, i: (bi, 0, 0)),
                  pl.BlockSpec((1, 1, d), lambda bi, i: (bi, 0, 1))],
        out_specs=pl.BlockSpec((1, tm, d), lambda bi, i: (bi, i, 0)),
        out_shape=jax.ShapeDtypeStruct((b, s, d), BF16),
        compiler_params=_params("arbitrary", "arbitrary"),
        name="norm_mod",
    )(x, g.reshape(1, d), mod, mod)


def _gates_kernel(h_ref, w_ref, b_ref, o_ref, *, silu_blocks):
    z = _dot(h_ref[0], w_ref[...]) + b_ref[...]
    sg = _sigmoid(z)
    is_silu = pl.program_id(2) < silu_blocks
    o_ref[0] = (sg * jnp.where(is_silu, z, 1.0)).astype(o_ref.dtype)


def _gates(h, w, bias, silu_cols, tm, tn):
    b, s, d = h.shape
    n = w.shape[1]
    return pl.pallas_call(
        functools.partial(_gates_kernel, silu_blocks=silu_cols // tn),
        grid=(b, s // tm, n // tn),
        in_specs=[pl.BlockSpec((1, tm, d), lambda bi, i, j: (bi, i, 0)),
                  pl.BlockSpec((d, tn), lambda bi, i, j: (0, j)),
                  pl.BlockSpec((1, tn), lambda bi, i, j: (0, j))],
        out_specs=pl.BlockSpec((1, tm, tn), lambda bi, i, j: (bi, i, j)),
        out_shape=jax.ShapeDtypeStruct((b, s, n), BF16),
        compiler_params=_params("arbitrary", "arbitrary", "arbitrary"),
        name="gate_proj",
    )(h, w, bias.reshape(1, n))


def _rms(x, g):
    return (x * lax.rsqrt(jnp.mean(x * x, axis=-1, keepdims=True) + EPS)) * g


def _mla_proj_kernel(h_ref, w_ref, b_ref, qg_ref, kvg_ref, wuq_ref, wukv_ref, cos_ref, sin_ref,
                     q_ref, k_ref, vt_ref, *, q_scale):
    z = _dot(h_ref[0], w_ref[...]) + b_ref[...]
    cos = cos_ref[...]
    sin = sin_ref[...]
    cq = _rms(z[:, :MLA_RANK], qg_ref[...]).astype(BF16)
    ckv = _rms(z[:, MLA_RANK:2 * MLA_RANK], kvg_ref[...]).astype(BF16)
    k_pe = _rope_tile(z[:, 2 * MLA_RANK:], cos, sin).astype(k_ref.dtype)
    q = _dot(cq, wuq_ref[...])
    kv = _dot(ckv, wukv_ref[...])
    for hd in range(MLA_HEADS):
        base = hd * MLA_QK_PAD
        q_ref[0, :, base:base + LANE] = (q[:, base:base + LANE] * q_scale).astype(q_ref.dtype)
        q_rope = _rope_tile(q[:, base + LANE:base + 2 * LANE], cos, sin)
        q_ref[0, :, base + LANE:base + 2 * LANE] = (q_rope * q_scale).astype(q_ref.dtype)
        k_ref[0, :, base:base + LANE] = kv[:, base:base + LANE].astype(k_ref.dtype)
        k_ref[0, :, base + LANE:base + 2 * LANE] = k_pe
        vt_ref[0, 0, hd * MLA_V:(hd + 1) * MLA_V, :] = kv[:, base + LANE:base + 2 * LANE].T.astype(vt_ref.dtype)


def _mla_proj(h, w, bias, qg, kvg, wuq, wukv, cos, sin, tm, tkc):
    b, s, d = h.shape
    n = w.shape[1]
    nq = MLA_HEADS * MLA_QK_PAD
    nv = MLA_HEADS * MLA_V
    per_chunk = tkc // tm
    const = lambda bi, i: (0, 0)
    return pl.pallas_call(
        functools.partial(_mla_proj_kernel, q_scale=float(MLA_QK) ** -0.5 * LOG2E),
        grid=(b, s // tm),
        in_specs=[pl.BlockSpec((1, tm, d), lambda bi, i: (bi, i, 0)),
                  pl.BlockSpec((d, n), const),
                  pl.BlockSpec((1, n), const),
                  pl.BlockSpec((1, MLA_RANK), const),
                  pl.BlockSpec((1, MLA_RANK), const),
                  pl.BlockSpec((MLA_RANK, nq), const),
                  pl.BlockSpec((MLA_RANK, nq), const),
                  pl.BlockSpec((tm, LANE), lambda bi, i: (i, 0)),
                  pl.BlockSpec((tm, LANE), lambda bi, i: (i, 0))],
        out_specs=[pl.BlockSpec((1, tm, nq), lambda bi, i: (bi, i, 0)),
                   pl.BlockSpec((1, tm, nq), lambda bi, i: (bi, i, 0)),
                   pl.BlockSpec((1, 1, nv, tm), lambda bi, i: (bi, i // per_chunk, 0, i % per_chunk))],
        out_shape=[jax.ShapeDtypeStruct((b, s, nq), BF16),
                   jax.ShapeDtypeStruct((b, s, nq), BF16),
                   jax.ShapeDtypeStruct((b, s // tkc, nv, tkc), BF16)],
        compiler_params=_params("arbitrary", "arbitrary"),
        name="mla_proj",
    )(h, w, bias.reshape(1, n), qg.reshape(1, -1), kvg.reshape(1, -1), wuq, wukv, cos, sin)


def _swa_proj_kernel(h_ref, w_ref, b_ref, cos_ref, sin_ref, q_ref, k_ref, v_ref, *, nq, nk, q_scale):
    z = _dot(h_ref[0], w_ref[...]) + b_ref[...]
    cos = cos_ref[...]
    sin = sin_ref[...]
    for t in range(nq // LANE):
        q_t = _rope_tile(z[:, t * LANE:(t + 1) * LANE], cos, sin) * q_scale
        q_ref[0, :, t * LANE:(t + 1) * LANE] = q_t.astype(q_ref.dtype)
    for t in range(nk // LANE):
        k_t = _rope_tile(z[:, nq + t * LANE:nq + (t + 1) * LANE], cos, sin)
        k_ref[0, :, t * LANE:(t + 1) * LANE] = k_t.astype(k_ref.dtype)
    v_ref[0] = z[:, nq + nk:].astype(v_ref.dtype)


def _swa_proj(h, w, bias, cos, sin, tm):
    b, s, d = h.shape
    n = w.shape[1]
    nq = SWA_HEADS * SWA_DIM
    nk = SWA_KV_HEADS * LANE
    const = lambda bi, i: (0, 0)
    return pl.pallas_call(
        functools.partial(_swa_proj_kernel, nq=nq, nk=nk, q_scale=float(SWA_DIM) ** -0.5 * LOG2E),
        grid=(b, s // tm),
        in_specs=[pl.BlockSpec((1, tm, d), lambda bi, i: (bi, i, 0)),
                  pl.BlockSpec((d, n), const),
                  pl.BlockSpec((1, n), const),
                  pl.BlockSpec((tm, LANE), lambda bi, i: (i, 0)),
                  pl.BlockSpec((tm, LANE), lambda bi, i: (i, 0))],
        out_specs=[pl.BlockSpec((1, tm, nq), lambda bi, i: (bi, i, 0)),
                   pl.BlockSpec((1, tm, nk), lambda bi, i: (bi, i, 0)),
                   pl.BlockSpec((1, tm, nk), lambda bi, i: (bi, i, 0))],
        out_shape=[jax.ShapeDtypeStruct((b, s, nq), BF16),
                   jax.ShapeDtypeStruct((b, s, nk), BF16),
                   jax.ShapeDtypeStruct((b, s, nk), BF16)],
        compiler_params=_params("arbitrary", "arbitrary"),
        name="swa_proj",
    )(h, w, bias.reshape(1, n), cos, sin)


def _mla_attn_kernel(q_ref, qn_ref, k_ref, vt_ref, kc_ref, vtc_ref, o_ref, s_a, s_b, acc_ref, cmax_ref,
                     *, tk, sub):
    tq = q_ref.shape[1]
    n_lat = k_ref.shape[1] // tk
    c = kc_ref.shape[1]
    s_bufs = (s_a, s_b)

    def pieces(n, with_ctx=False):
        out = []
        for lo in range(0, tk, sub):
            hi = min(lo + sub, tk)
            out.append((slice(lo, hi),
                        lambda lo=lo, hi=hi: k_ref[0, pl.ds(pl.multiple_of(n * tk, tk) + lo, hi - lo), :],
                        lambda lo=lo, hi=hi: vt_ref[0, n, :, lo:hi]))
        if with_ctx:
            out.append((slice(tk, tk + c), lambda: kc_ref[0], lambda: vtc_ref[0, 0]))
        return out

    def fused_step(m, l, score=None, soft=None):
        if soft is not None:
            m_new = jnp.maximum(m, soft[2])
            alpha = jnp.exp2(m - m_new)
        cmax = lsum = pv = None
        n_score = len(score[0]) if score is not None else 0
        n_soft = len(soft[0]) if soft is not None else 0
        for j in range(max(n_score, n_soft)):
            if j < n_score:
                rows, k_load, _ = score[0][j]
                s = _dot_nt(k_load(), score[2][0])
                score[1][rows, :] = s
                piece_max = jnp.max(s, axis=0, keepdims=True)
                cmax = piece_max if cmax is None else jnp.maximum(cmax, piece_max)
            if j < n_soft:
                rows, _, vt_load = soft[0][j]
                p = jnp.exp2(soft[1][rows, :] - m_new)
                piece_sum = jnp.sum(p, axis=0, keepdims=True)
                lsum = piece_sum if lsum is None else lsum + piece_sum
                d = _dot(vt_load(), p.astype(BF16))
                pv = d if pv is None else pv + d
        if soft is not None:
            acc_ref[...] = alpha * acc_ref[...] + pv
            return m_new, alpha * l + lsum, cmax
        return m, l, cmax

    @pl.when(pl.program_id(2) == 0)
    def _():
        cmax_ref[...] = fused_step(None, None, score=(pieces(0), s_bufs[0], q_ref))[2]

    acc_ref[...] = jnp.zeros_like(acc_ref)
    m = jnp.full((1, tq), -jnp.inf, F32)
    l = jnp.zeros((1, tq), F32)
    cmax = cmax_ref[...]

    def step(n, u, carry, last=False):
        m, l, cmax = carry
        return fused_step(m, l, score=(pieces(n, last), s_bufs[u], q_ref),
                          soft=(pieces(n - 1), s_bufs[1 - u], cmax))

    def pair(t, carry):
        return step(2 * t + 2, 0, step(2 * t + 1, 1, carry))

    carry = lax.fori_loop(0, n_lat // 2 - 1, pair, (m, l, cmax))
    m, l, cmax = step(n_lat - 1, 1, carry, last=True)
    m, l, cmax_next = fused_step(m, l, score=(pieces(0), s_bufs[0], qn_ref),
                                 soft=(pieces(n_lat - 1, True), s_bufs[1], cmax))
    cmax_ref[...] = cmax_next
    o_ref[0] = (acc_ref[...] / l).T.astype(o_ref.dtype)


def _mla_attn(q, k, vt, kc, vtc, tq, sub):
    b, s, _ = q.shape
    c = kc.shape[1]
    n_chunks, _, tk = vt.shape[1:]
    n_q = s // tq
    assert n_chunks % 2 == 0 and n_chunks >= 2
    return pl.pallas_call(
        functools.partial(_mla_attn_kernel, tk=tk, sub=sub),
        grid=(b, MLA_HEADS, n_q),
        in_specs=[pl.BlockSpec((1, tq, MLA_QK_PAD), lambda bi, hd, i: (bi, i, hd)),
                  pl.BlockSpec((1, tq, MLA_QK_PAD), lambda bi, hd, i: (bi, jnp.minimum(i + 1, n_q - 1), hd)),
                  pl.BlockSpec((1, s, MLA_QK_PAD), lambda bi, hd, i: (bi, 0, hd)),
                  pl.BlockSpec((1, n_chunks, MLA_V, tk), lambda bi, hd, i: (bi, 0, hd, 0)),
                  pl.BlockSpec((1, c, MLA_QK_PAD), lambda bi, hd, i: (bi, 0, hd)),
                  pl.BlockSpec((1, 1, MLA_V, c), lambda bi, hd, i: (bi, 0, hd, 0))],
        out_specs=pl.BlockSpec((1, tq, MLA_V), lambda bi, hd, i: (bi, i, hd)),
        out_shape=jax.ShapeDtypeStruct((b, s, MLA_HEADS * MLA_V), BF16),
        scratch_shapes=[pltpu.VMEM((tk + c, tq), F32), pltpu.VMEM((tk + c, tq), F32),
                        pltpu.VMEM((MLA_V, tq), F32), pltpu.VMEM((1, tq), F32)],
        compiler_params=_params("arbitrary", "arbitrary", "arbitrary"),
        name="mla_attn",
    )(q, q, k, vt, kc, vtc)


def _swa_attn_kernel(sink_ref, q_ref, k_ref, v_ref, kc_ref, vc_ref, o_ref, *, tq):
    kvh = pl.program_id(1)
    i = pl.program_id(2)
    s_len = k_ref.shape[1]
    blk = LANE
    span = blk + 2 * WINDOW
    rows = SWA_GROUP * blk
    kc = kc_ref[0]
    vc = vc_ref[0]
    low_half = lax.broadcasted_iota(jnp.int32, (blk, LANE), 1) < SWA_DIM
    row_id = lax.broadcasted_iota(jnp.int32, (rows, 1), 0)
    head_of_row = row_id // blk
    sink = jnp.zeros((rows, 1), F32)
    for g in range(SWA_GROUP):
        sink = jnp.where(head_of_row == g, sink_ref[kvh * SWA_GROUP + g] * LOG2E, sink)
    for r in range(tq // blk):
        q0 = i * tq + r * blk
        start = pl.multiple_of(jnp.clip(q0 - WINDOW, 0, s_len - span), LANE)
        kw = k_ref[0, pl.ds(start, span), :]
        vw = v_ref[0, pl.ds(start, span), :]
        qpos = q0 + lax.broadcasted_iota(jnp.int32, (rows, span), 0) % blk
        kpos = start + lax.broadcasted_iota(jnp.int32, (rows, span), 1)
        band = jnp.abs(qpos - kpos) <= WINDOW
        q_rows = []
        for g in range(SWA_GROUP):
            q_t = q_ref[0, r * blk:(r + 1) * blk, (g // 2) * LANE:(g // 2 + 1) * LANE]
            keep = low_half if g % 2 == 0 else jnp.logical_not(low_half)
            q_rows.append(jnp.where(keep, q_t, jnp.zeros_like(q_t)))
        q4 = jnp.concatenate(q_rows, axis=0)
        s_w = jnp.where(band, _dot_nt(q4, kw), NEG)
        s_c = _dot_nt(q4, kc)
        m = jnp.maximum(jnp.maximum(jnp.max(s_w, axis=1, keepdims=True),
                                    jnp.max(s_c, axis=1, keepdims=True)), sink)
        p_w = jnp.exp2(s_w - m)
        p_c = jnp.exp2(s_c - m)
        l = jnp.sum(p_w, axis=1, keepdims=True) + jnp.sum(p_c, axis=1, keepdims=True) + jnp.exp2(sink - m)
        o = (_dot(p_w.astype(vw.dtype), vw) + _dot(p_c.astype(vc.dtype), vc)) / l
        for t in range(SWA_GROUP // 2):
            pair = jnp.where(low_half, o[2 * t * blk:(2 * t + 1) * blk], o[(2 * t + 1) * blk:(2 * t + 2) * blk])
            o_ref[0, r * blk:(r + 1) * blk, t * LANE:(t + 1) * LANE] = pair.astype(o_ref.dtype)


def _swa_attn(sink, q, k, v, kc, vc, tq):
    b, s, _ = q.shape
    c = kc.shape[1]
    gw = SWA_GROUP * SWA_DIM
    return pl.pallas_call(
        functools.partial(_swa_attn_kernel, tq=tq),
        grid=(b, SWA_KV_HEADS, s // tq),
        in_specs=[pl.BlockSpec(memory_space=pltpu.SMEM),
                  pl.BlockSpec((1, tq, gw), lambda bi, kh, i: (bi, i, kh)),
                  pl.BlockSpec((1, s, LANE), lambda bi, kh, i: (bi, 0, kh)),
                  pl.BlockSpec((1, s, LANE), lambda bi, kh, i: (bi, 0, kh)),
                  pl.BlockSpec((1, c, LANE), lambda bi, kh, i: (bi, 0, kh)),
                  pl.BlockSpec((1, c, LANE), lambda bi, kh, i: (bi, 0, kh))],
        out_specs=pl.BlockSpec((1, tq, gw), lambda bi, kh, i: (bi, i, kh)),
        out_shape=jax.ShapeDtypeStruct((b, s, SWA_HEADS * SWA_DIM), BF16),
        compiler_params=_params("arbitrary", "arbitrary", "arbitrary"),
        name="swa_attn",
    )(sink, q, k, v, kc, vc)


def _merge_kernel(a_ref, b_ref, ga_ref, gb_ref, mga_ref, mgb_ref, x_ref, gate_ref,
                  wba_ref, wbb_ref, wo_ref, fg_ref, o_ref):
    a_in = (a_ref[0].astype(F32) * ga_ref[0].astype(F32)).astype(BF16)
    b_in = (b_ref[0].astype(F32) * gb_ref[0].astype(F32)).astype(BF16)
    y_a = _dot(a_in, wba_ref[...])
    y_b = _dot(b_in, wbb_ref[...])
    mix = mga_ref[0].astype(F32) * y_a + mgb_ref[0].astype(F32) * y_b
    out = _dot(mix.astype(BF16), wo_ref[...])
    xn = x_ref[0] + gate_ref[0] * out
    o_ref[0] = _rms(xn, fg_ref[...]).astype(o_ref.dtype)


def _merge(a, bh, gates, x, mod, wba, wbb, wo, fg, tm):
    b, s, d = x.shape
    wa = a.shape[2]
    const = lambda bi, i: (0, 0)
    row = lambda col: (lambda bi, i: (bi, i, col))
    return pl.pallas_call(
        _merge_kernel,
        grid=(b, s // tm),
        in_specs=[pl.BlockSpec((1, tm, wa), row(0)),
                  pl.BlockSpec((1, tm, wa), row(0)),
                  pl.BlockSpec((1, tm, wa), row(0)),
                  pl.BlockSpec((1, tm, wa), row(1)),
                  pl.BlockSpec((1, tm, d), row(1)),
                  pl.BlockSpec((1, tm, d), row(2)),
                  pl.BlockSpec((1, tm, d), row(0)),
                  pl.BlockSpec((1, 1, d), lambda bi, i: (bi, 0, 2)),
                  pl.BlockSpec((wa, d), const),
                  pl.BlockSpec((wa, d), const),
                  pl.BlockSpec((d, d), const),
                  pl.BlockSpec((1, d), const)],
        out_specs=pl.BlockSpec((1, tm, d), row(0)),
        out_shape=jax.ShapeDtypeStruct((b, s, d), x.dtype),
        compiler_params=_params("arbitrary", "arbitrary"),
        name="merge_out",
    )(a, bh, gates, gates, gates, gates, x, mod, wba, wbb, wo, fg.reshape(1, d))


def _rope_tables(s):
    t = jnp.arange(s, dtype=jnp.int32)
    n = 16
    inv = ROPE_BASE ** (-jnp.arange(n, dtype=F32) / n)
    ang_r = (t // GRID_W).astype(F32)[:, None] * inv[None, :]
    ang_c = (t % GRID_W).astype(F32)[:, None] * inv[None, :]
    cos = jnp.concatenate([jnp.cos(ang_r)] * 2 + [jnp.cos(ang_c)] * 2, axis=1)
    sin = jnp.concatenate([-jnp.sin(ang_r), jnp.sin(ang_r), -jnp.sin(ang_c), jnp.sin(ang_c)], axis=1)
    return jnp.tile(cos, (1, 2)), jnp.tile(sin, (1, 2))


def _dup_heads(w, heads, dim):
    lead = w.shape[:-1]
    w = w.reshape(lead + (heads, 1, dim))
    return jnp.broadcast_to(w, lead + (heads, 2, dim)).reshape(lead + (heads * 2 * dim,))


def kernel(x, c, ctx, c_ctx, w_ada, b_ada, norm_g, w_in, b_in, q_norm_g, kv_norm_g, w_uq, w_ukv, sink,
           w_branch_a, w_branch_b, w_out, final_g):
    bsz, s, d = x.shape
    c_len = ctx.shape[1]
    assert w_ada.shape[0] == 1, "single-layer block"
    widths = (MLA_RANK, MLA_RANK, MLA_ROPE, MLA_HEADS * MLA_V, SWA_HEADS * SWA_DIM,
              SWA_KV_HEADS * SWA_DIM, SWA_KV_HEADS * SWA_DIM, SWA_HEADS * SWA_DIM, d, d)
    assert sum(widths) == w_in.shape[2]
    offs = [0]
    for w_ in widths:
        offs.append(offs[-1] + w_)
    w_in0, b_in0 = w_in[0], b_in[0]
    col = lambda k: (w_in0[:, offs[k]:offs[k + 1]], b_in0[offs[k]:offs[k + 1]])
    (w_cq, b_cq), (w_ckv, b_ckv), (w_kpe, b_kpe), (w_ga, b_ga), (w_qs, b_qs) = (col(k) for k in range(5))
    (w_ks, b_ks), (w_vs, b_vs), (w_gb, b_gb), (w_mga, b_mga), (w_mgb, b_mgb) = (col(k) for k in range(5, 10))

    rows = 8
    c_rows = jnp.concatenate([c, c_ctx[None, :], jnp.zeros((rows - bsz - 1, d), F32)], axis=0)
    mod = _modulation(c_rows, w_ada[0], b_ada[0])
    mod_lat = mod[:bsz].reshape(bsz, 1, 3 * d)
    mod_ctx = jnp.broadcast_to(mod[bsz].reshape(1, 1, 3 * d), (bsz, 1, 3 * d))

    h = _norm_mod(x, norm_g[0], mod_lat, NORM_TM)
    h_c = _norm_mod(ctx, norm_g[0], mod_ctx, c_len)

    pad_pe = LANE - MLA_ROPE
    w_mla = jnp.concatenate([w_cq, w_ckv, w_kpe, jnp.zeros((d, pad_pe), F32)], axis=1).astype(BF16)
    b_mla = jnp.concatenate([b_cq, b_ckv, b_kpe, jnp.zeros((pad_pe,), F32)])
    wuq = jnp.pad(w_uq[0].reshape(MLA_RANK, MLA_HEADS, MLA_QK),
                  ((0, 0), (0, 0), (0, MLA_QK_PAD - MLA_QK))).reshape(MLA_RANK, -1).astype(BF16)
    wukv = w_ukv[0].astype(BF16)
    w_swa = jnp.concatenate([w_qs, _dup_heads(w_ks, SWA_KV_HEADS, SWA_DIM),
                             _dup_heads(w_vs, SWA_KV_HEADS, SWA_DIM)], axis=1).astype(BF16)
    b_swa = jnp.concatenate([b_qs, _dup_heads(b_ks, SWA_KV_HEADS, SWA_DIM),
                             _dup_heads(b_vs, SWA_KV_HEADS, SWA_DIM)])
    w_gates = jnp.concatenate([w_ga, w_gb, w_mga, w_mgb], axis=1).astype(BF16)
    b_gates = jnp.concatenate([b_ga, b_gb, b_mga, b_mgb])

    cos, sin = _rope_tables(s)
    cos_c = jnp.ones((c_len, LANE), F32)
    sin_c = jnp.zeros((c_len, LANE), F32)

    q_mla, k_mla, vt_mla = _mla_proj(h, w_mla, b_mla, q_norm_g[0], kv_norm_g[0], wuq, wukv, cos, sin,
                                     PROJ_TM, MLA_TK)
    _, kc_mla, vtc_mla = _mla_proj(h_c, w_mla, b_mla, q_norm_g[0], kv_norm_g[0], wuq, wukv, cos_c, sin_c,
                                   c_len, c_len)
    q_s, k_s, v_s = _swa_proj(h, w_swa, b_swa, cos, sin, PROJ_TM)
    _, kc_s, vc_s = _swa_proj(h_c, w_swa, b_swa, cos_c, sin_c, c_len)
    gates = _gates(h, w_gates, b_gates, 2 * MLA_HEADS * MLA_V, GATE_TM, GATE_TN)

    a_heads = _mla_attn(q_mla, k_mla, vt_mla, kc_mla, vtc_mla, MLA_TQ, MLA_SUB)
    b_heads = _swa_attn(sink[0], q_s, k_s, v_s, kc_s, vc_s, SWA_TQ)

    return _merge(a_heads, b_heads, gates, x, mod_lat, w_branch_a[0].astype(BF16),
                  w_branch_b[0].astype(BF16), w_out[0].astype(BF16), final_g, MERGE_TM)
```

```python
import functools

import jax
import jax.numpy as jnp
from jax import lax
from jax.experimental import pallas as pl
from jax.experimental.pallas import tpu as pltpu

F32 = jnp.float32
BF16 = jnp.bfloat16

LANE = 128
EPS = 1e-6
NEG = -1e30
ROPE_BASE = 10000.0
GRID_W = 64
LOG2E = 1.4426950408889634
WINDOW = 128

MLA_HEADS = 8
MLA_NOPE = 128
MLA_ROPE = 64
MLA_V = 128
MLA_QK = MLA_NOPE + MLA_ROPE
MLA_QK_PAD = 2 * LANE
MLA_RANK = 512
SWA_HEADS = 16
SWA_KV_HEADS = 4
SWA_GROUP = SWA_HEADS // SWA_KV_HEADS
SWA_DIM = 64

VMEM_LIMIT = 56 * 1024 * 1024

NORM_TM = 512
PROJ_TM = 512
GATE_TM, GATE_TN = 1024, 2048
MLA_TQ = 1024
MLA_TK = 2048
MLA_SUB = 512
SWA_TQ = 512
MERGE_TM = 256


def _params(*sem):
    return pltpu.CompilerParams(dimension_semantics=sem, vmem_limit_bytes=VMEM_LIMIT)


def _sigmoid(x):
    return 1.0 / (1.0 + jnp.exp(-x))


def _dot(a, b):
    return jnp.dot(a, b, preferred_element_type=F32)


def _dot_nt(a, b):
    return lax.dot_general(a, b, (((1,), (1,)), ((), ())), preferred_element_type=F32)


def _rope_tile(x, cos, sin_signed):
    lane = lax.broadcasted_iota(jnp.int32, x.shape, 1)
    first = (lane % 32) < 16
    partner = jnp.where(first, pltpu.roll(x, LANE - 16, 1), pltpu.roll(x, 16, 1))
    return x * cos + partner * sin_signed


def _split_bf16(a):
    hi = a.astype(BF16)
    lo = (a - hi.astype(F32)).astype(BF16)
    return hi, lo


def _mod_kernel(c_ref, w_ref, b_ref, o_ref):
    c = c_ref[...]
    s = c * _sigmoid(c)
    s_hi, s_lo = _split_bf16(s)
    w_hi, w_lo = _split_bf16(w_ref[...])
    acc = _dot(s_hi, w_hi) + (_dot(s_hi, w_lo) + _dot(s_lo, w_hi))
    o_ref[...] = acc + b_ref[...]


def _modulation(c_rows, w_ada, b_ada):
    rows, d = c_rows.shape
    n = w_ada.shape[1]
    tn = 1536
    return pl.pallas_call(
        _mod_kernel,
        grid=(n // tn,),
        in_specs=[pl.BlockSpec((rows, d), lambda j: (0, 0)),
                  pl.BlockSpec((d, tn), lambda j: (0, j)),
                  pl.BlockSpec((1, tn), lambda j: (0, j))],
        out_specs=pl.BlockSpec((rows, tn), lambda j: (0, j)),
        out_shape=jax.ShapeDtypeStruct((rows, n), F32),
        compiler_params=_params("arbitrary"),
        name="adaln_mod",
    )(c_rows, w_ada, b_ada.reshape(1, n))


def _norm_mod_kernel(x_ref, g_ref, shift_ref, scale_ref, h_ref):
    x = x_ref[0]
    y = x * lax.rsqrt(jnp.mean(x * x, axis=-1, keepdims=True) + EPS)
    h = (y * g_ref[...]) * (1.0 + scale_ref[0]) + shift_ref[0]
    h_ref[0] = h.astype(h_ref.dtype)


def _norm_mod(x, g, mod, tm):
    b, s, d = x.shape
    return pl.pallas_call(
        _norm_mod_kernel,
        grid=(b, s // tm),
        in_specs=[pl.BlockSpec((1, tm, d), lambda bi, i: (bi, i, 0)),
                  pl.BlockSpec((1, d), lambda bi, i: (0, 0)),
                  pl.BlockSpec((1, 1, d), lambda bi, i: (bi, 0, 0)),
                  pl.BlockSpec((1, 1, d), lambda bi, i: (bi, 0, 1))],
        out_specs=pl.BlockSpec((1, tm, d), lambda bi, i: (bi, i, 0)),
        out_shape=jax.ShapeDtypeStruct((b, s, d), BF16),
        compiler_params=_params("arbitrary", "arbitrary"),
        name="norm_mod",
    )(x, g.reshape(1, d), mod, mod)


def _gates_kernel(h_ref, w_ref, b_ref, o_ref, *, silu_blocks):
    z = _dot(h_ref[0], w_ref[...]) + b_ref[...]
    sg = _sigmoid(z)
    is_silu = pl.program_id(2) < silu_blocks
    o_ref[0] = (sg * jnp.where(is_silu, z, 1.0)).astype(o_ref.dtype)


def _gates(h, w, bias, silu_cols, tm, tn):
    b, s, d = h.shape
    n = w.shape[1]
    return pl.pallas_call(
        functools.partial(_gates_kernel, silu_blocks=silu_cols // tn),
        grid=(b, s // tm, n // tn),
        in_specs=[pl.BlockSpec((1, tm, d), lambda bi, i, j: (bi, i, 0)),
                  pl.BlockSpec((d, tn), lambda bi, i, j: (0, j)),
                  pl.BlockSpec((1, tn), lambda bi, i, j: (0, j))],
        out_specs=pl.BlockSpec((1, tm, tn), lambda bi, i, j: (bi, i, j)),
        out_shape=jax.ShapeDtypeStruct((b, s, n), BF16),
        compiler_params=_params("arbitrary", "arbitrary", "arbitrary"),
        name="gate_proj",
    )(h, w, bias.reshape(1, n))


def _rms(x, g):
    return (x * lax.rsqrt(jnp.mean(x * x, axis=-1, keepdims=True) + EPS)) * g


def _mla_proj_kernel(h_ref, w_ref, b_ref, qg_ref, kvg_ref, wuq_ref, wukv_ref, cos_ref, sin_ref,
                     q_ref, k_ref, vt_ref, *, q_scale):
    z = _dot(h_ref[0], w_ref[...]) + b_ref[...]
    cos = cos_ref[...]
    sin = sin_ref[...]
    cq = _rms(z[:, :MLA_RANK], qg_ref[...]).astype(BF16)
    ckv = _rms(z[:, MLA_RANK:2 * MLA_RANK], kvg_ref[...]).astype(BF16)
    k_pe = _rope_tile(z[:, 2 * MLA_RANK:], cos, sin).astype(k_ref.dtype)
    q = _dot(cq, wuq_ref[...])
    kv = _dot(ckv, wukv_ref[...])
    for hd in range(MLA_HEADS):
        base = hd * MLA_QK_PAD
        q_rope = _rope_tile(q[:, base + LANE:base + 2 * LANE], cos, sin)
        q_ref[0, base:base + LANE, :] = (q[:, base:base + LANE] * q_scale).T.astype(q_ref.dtype)
        q_ref[0, base + LANE:base + 2 * LANE, :] = (q_rope * q_scale).T.astype(q_ref.dtype)
        k_ref[0, :, base:base + LANE] = kv[:, base:base + LANE].astype(k_ref.dtype)
        k_ref[0, :, base + LANE:base + 2 * LANE] = k_pe
        vt_ref[0, 0, hd * MLA_V:(hd + 1) * MLA_V, :] = kv[:, base + LANE:base + 2 * LANE].T.astype(vt_ref.dtype)


def _mla_proj(h, w, bias, qg, kvg, wuq, wukv, cos, sin, tm, tkc):
    b, s, d = h.shape
    n = w.shape[1]
    nq = MLA_HEADS * MLA_QK_PAD
    nv = MLA_HEADS * MLA_V
    per_chunk = tkc // tm
    const = lambda bi, i: (0, 0)
    return pl.pallas_call(
        functools.partial(_mla_proj_kernel, q_scale=float(MLA_QK) ** -0.5 * LOG2E),
        grid=(b, s // tm),
        in_specs=[pl.BlockSpec((1, tm, d), lambda bi, i: (bi, i, 0)),
                  pl.BlockSpec((d, n), const),
                  pl.BlockSpec((1, n), const),
                  pl.BlockSpec((1, MLA_RANK), const),
                  pl.BlockSpec((1, MLA_RANK), const),
                  pl.BlockSpec((MLA_RANK, nq), const),
                  pl.BlockSpec((MLA_RANK, nq), const),
                  pl.BlockSpec((tm, LANE), lambda bi, i: (i, 0)),
                  pl.BlockSpec((tm, LANE), lambda bi, i: (i, 0))],
        out_specs=[pl.BlockSpec((1, nq, tm), lambda bi, i: (bi, 0, i)),
                   pl.BlockSpec((1, tm, nq), lambda bi, i: (bi, i, 0)),
                   pl.BlockSpec((1, 1, nv, tm), lambda bi, i: (bi, i // per_chunk, 0, i % per_chunk))],
        out_shape=[jax.ShapeDtypeStruct((b, nq, s), BF16),
                   jax.ShapeDtypeStruct((b, s, nq), BF16),
                   jax.ShapeDtypeStruct((b, s // tkc, nv, tkc), BF16)],
        compiler_params=_params("arbitrary", "arbitrary"),
        name="mla_proj",
    )(h, w, bias.reshape(1, n), qg.reshape(1, -1), kvg.reshape(1, -1), wuq, wukv, cos, sin)


def _swa_proj_kernel(h_ref, w_ref, b_ref, cos_ref, sin_ref, q_ref, k_ref, v_ref, *, nq, nk, q_scale):
    z = _dot(h_ref[0], w_ref[...]) + b_ref[...]
    cos = cos_ref[...]
    sin = sin_ref[...]
    for t in range(nq // LANE):
        q_t = _rope_tile(z[:, t * LANE:(t + 1) * LANE], cos, sin) * q_scale
        q_ref[0, :, t * LANE:(t + 1) * LANE] = q_t.astype(q_ref.dtype)
    for t in range(nk // LANE):
        k_t = _rope_tile(z[:, nq + t * LANE:nq + (t + 1) * LANE], cos, sin)
        k_ref[0, :, t * LANE:(t + 1) * LANE] = k_t.astype(k_ref.dtype)
    v_ref[0] = z[:, nq + nk:].astype(v_ref.dtype)


def _swa_proj(h, w, bias, cos, sin, tm):
    b, s, d = h.shape
    n = w.shape[1]
    nq = SWA_HEADS * SWA_DIM
    nk = SWA_KV_HEADS * LANE
    const = lambda bi, i: (0, 0)
    return pl.pallas_call(
        functools.partial(_swa_proj_kernel, nq=nq, nk=nk, q_scale=float(SWA_DIM) ** -0.5 * LOG2E),
        grid=(b, s // tm),
        in_specs=[pl.BlockSpec((1, tm, d), lambda bi, i: (bi, i, 0)),
                  pl.BlockSpec((d, n), const),
                  pl.BlockSpec((1, n), const),
                  pl.BlockSpec((tm, LANE), lambda bi, i: (i, 0)),
                  pl.BlockSpec((tm, LANE), lambda bi, i: (i, 0))],
        out_specs=[pl.BlockSpec((1, tm, nq), lambda bi, i: (bi, i, 0)),
                   pl.BlockSpec((1, tm, nk), lambda bi, i: (bi, i, 0)),
                   pl.BlockSpec((1, tm, nk), lambda bi, i: (bi, i, 0))],
        out_shape=[jax.ShapeDtypeStruct((b, s, nq), BF16),
                   jax.ShapeDtypeStruct((b, s, nk), BF16),
                   jax.ShapeDtypeStruct((b, s, nk), BF16)],
        compiler_params=_params("arbitrary", "arbitrary"),
        name="swa_proj",
    )(h, w, bias.reshape(1, n), cos, sin)


def _mla_attn_kernel(q_ref, qn_ref, k_ref, vt_ref, kc_ref, vtc_ref, o_ref, s_a, s_b, acc_ref, cmax_ref,
                     *, tk, sub):
    tq = q_ref.shape[2]
    n_lat = k_ref.shape[1] // tk
    c = kc_ref.shape[1]
    s_bufs = (s_a, s_b)

    def pieces(n, with_ctx=False):
        out = []
        for lo in range(0, tk, sub):
            hi = min(lo + sub, tk)
            out.append((slice(lo, hi),
                        lambda lo=lo, hi=hi: k_ref[0, pl.ds(pl.multiple_of(n * tk, tk) + lo, hi - lo), :],
                        lambda lo=lo, hi=hi: vt_ref[0, n, :, lo:hi]))
        if with_ctx:
            out.append((slice(tk, tk + c), lambda: kc_ref[0], lambda: vtc_ref[0, 0]))
        return out

    def fused_step(m, l, score=None, soft=None):
        if soft is not None:
            m_new = jnp.maximum(m, soft[2])
            alpha = jnp.exp2(m - m_new)
        cmax = lsum = pv = None
        n_score = len(score[0]) if score is not None else 0
        n_soft = len(soft[0]) if soft is not None else 0
        for j in range(max(n_score, n_soft)):
            if j < n_score:
                rows, k_load, _ = score[0][j]
                s = _dot(k_load(), score[2][0])
                score[1][rows, :] = s
                piece_max = jnp.max(s, axis=0, keepdims=True)
                cmax = piece_max if cmax is None else jnp.maximum(cmax, piece_max)
            if j < n_soft:
                rows, _, vt_load = soft[0][j]
                p = jnp.exp2(soft[1][rows, :] - m_new)
                piece_sum = jnp.sum(p, axis=0, keepdims=True)
                lsum = piece_sum if lsum is None else lsum + piece_sum
                d = _dot(vt_load(), p.astype(BF16))
                pv = d if pv is None else pv + d
        if soft is not None:
            acc_ref[...] = alpha * acc_ref[...] + pv
            return m_new, alpha * l + lsum, cmax
        return m, l, cmax

    @pl.when(pl.program_id(2) == 0)
    def _():
        cmax_ref[...] = fused_step(None, None, score=(pieces(0), s_bufs[0], q_ref))[2]

    acc_ref[...] = jnp.zeros_like(acc_ref)
    m = jnp.full((1, tq), -jnp.inf, F32)
    l = jnp.zeros((1, tq), F32)
    cmax = cmax_ref[...]

    def step(n, u, carry, last=False):
        m, l, cmax = carry
        return fused_step(m, l, score=(pieces(n, last), s_bufs[u], q_ref),
                          soft=(pieces(n - 1), s_bufs[1 - u], cmax))

    def pair(t, carry):
        return step(2 * t + 2, 0, step(2 * t + 1, 1, carry))

    carry = lax.fori_loop(0, n_lat // 2 - 1, pair, (m, l, cmax))
    m, l, cmax = step(n_lat - 1, 1, carry, last=True)
    m, l, cmax_next = fused_step(m, l, score=(pieces(0), s_bufs[0], qn_ref),
                                 soft=(pieces(n_lat - 1, True), s_bufs[1], cmax))
    cmax_ref[...] = cmax_next
    o_ref[0] = (acc_ref[...] / l).T.astype(o_ref.dtype)


def _mla_attn(qt, k, vt, kc, vtc, tq, sub):
    b, _, s = qt.shape
    c = kc.shape[1]
    n_chunks, _, tk = vt.shape[1:]
    n_q = s // tq
    assert n_chunks % 2 == 0 and n_chunks >= 2
    return pl.pallas_call(
        functools.partial(_mla_attn_kernel, tk=tk, sub=sub),
        grid=(b, MLA_HEADS, n_q),
        in_specs=[pl.BlockSpec((1, MLA_QK_PAD, tq), lambda bi, hd, i: (bi, hd, i)),
                  pl.BlockSpec((1, MLA_QK_PAD, tq), lambda bi, hd, i: (bi, hd, jnp.minimum(i + 1, n_q - 1))),
                  pl.BlockSpec((1, s, MLA_QK_PAD), lambda bi, hd, i: (bi, 0, hd)),
                  pl.BlockSpec((1, n_chunks, MLA_V, tk), lambda bi, hd, i: (bi, 0, hd, 0)),
                  pl.BlockSpec((1, c, MLA_QK_PAD), lambda bi, hd, i: (bi, 0, hd)),
                  pl.BlockSpec((1, 1, MLA_V, c), lambda bi, hd, i: (bi, 0, hd, 0))],
        out_specs=pl.BlockSpec((1, tq, MLA_V), lambda bi, hd, i: (bi, i, hd)),
        out_shape=jax.ShapeDtypeStruct((b, s, MLA_HEADS * MLA_V), BF16),
        scratch_shapes=[pltpu.VMEM((tk + c, tq), F32), pltpu.VMEM((tk + c, tq), F32),
                        pltpu.VMEM((MLA_V, tq), F32), pltpu.VMEM((1, tq), F32)],
        compiler_params=_params("arbitrary", "arbitrary", "arbitrary"),
        name="mla_attn",
    )(qt, qt, k, vt, kc, vtc)


def _swa_attn_kernel(sink_ref, q_ref, k_ref, v_ref, kc_ref, vc_ref, o_ref, *, tq):
    kvh = pl.program_id(1)
    i = pl.program_id(2)
    s_len = k_ref.shape[1]
    blk = LANE
    span = blk + 2 * WINDOW
    rows = SWA_GROUP * blk
    kc = kc_ref[0]
    vc = vc_ref[0]
    low_half = lax.broadcasted_iota(jnp.int32, (blk, LANE), 1) < SWA_DIM
    row_id = lax.broadcasted_iota(jnp.int32, (rows, 1), 0)
    head_of_row = row_id // blk
    sink = jnp.zeros((rows, 1), F32)
    for g in range(SWA_GROUP):
        sink = jnp.where(head_of_row == g, sink_ref[kvh * SWA_GROUP + g] * LOG2E, sink)
    for r in range(tq // blk):
        q0 = i * tq + r * blk
        start = pl.multiple_of(jnp.clip(q0 - WINDOW, 0, s_len - span), LANE)
        kw = k_ref[0, pl.ds(start, span), :]
        vw = v_ref[0, pl.ds(start, span), :]
        qpos = q0 + lax.broadcasted_iota(jnp.int32, (rows, span), 0) % blk
        kpos = start + lax.broadcasted_iota(jnp.int32, (rows, span), 1)
        band = jnp.abs(qpos - kpos) <= WINDOW
        q_rows = []
        for g in range(SWA_GROUP):
            q_t = q_ref[0, r * blk:(r + 1) * blk, (g // 2) * LANE:(g // 2 + 1) * LANE]
            keep = low_half if g % 2 == 0 else jnp.logical_not(low_half)
            q_rows.append(jnp.where(keep, q_t, jnp.zeros_like(q_t)))
        q4 = jnp.concatenate(q_rows, axis=0)
        s_w = jnp.where(band, _dot_nt(q4, kw), NEG)
        s_c = _dot_nt(q4, kc)
        m = jnp.maximum(jnp.maximum(jnp.max(s_w, axis=1, keepdims=True),
                                    jnp.max(s_c, axis=1, keepdims=True)), sink)
        p_w = jnp.exp2(s_w - m)
        p_c = jnp.exp2(s_c - m)
        l = jnp.sum(p_w, axis=1, keepdims=True) + jnp.sum(p_c, axis=1, keepdims=True) + jnp.exp2(sink - m)
        o = (_dot(p_w.astype(vw.dtype), vw) + _dot(p_c.astype(vc.dtype), vc)) / l
        for t in range(SWA_GROUP // 2):
            pair = jnp.where(low_half, o[2 * t * blk:(2 * t + 1) * blk], o[(2 * t + 1) * blk:(2 * t + 2) * blk])
            o_ref[0, r * blk:(r + 1) * blk, t * LANE:(t + 1) * LANE] = pair.astype(o_ref.dtype)


def _swa_attn(sink, q, k, v, kc, vc, tq):
    b, s, _ = q.shape
    c = kc.shape[1]
    gw = SWA_GROUP * SWA_DIM
    return pl.pallas_call(
        functools.partial(_swa_attn_kernel, tq=tq),
        grid=(b, SWA_KV_HEADS, s // tq),
        in_specs=[pl.BlockSpec(memory_space=pltpu.SMEM),
                  pl.BlockSpec((1, tq, gw), lambda bi, kh, i: (bi, i, kh)),
                  pl.BlockSpec((1, s, LANE), lambda bi, kh, i: (bi, 0, kh)),
                  pl.BlockSpec((1, s, LANE), lambda bi, kh, i: (bi, 0, kh)),
                  pl.BlockSpec((1, c, LANE), lambda bi, kh, i: (bi, 0, kh)),
                  pl.BlockSpec((1, c, LANE), lambda bi, kh, i: (bi, 0, kh))],
        out_specs=pl.BlockSpec((1, tq, gw), lambda bi, kh, i: (bi, i, kh)),
        out_shape=jax.ShapeDtypeStruct((b, s, SWA_HEADS * SWA_DIM), BF16),
        compiler_params=_params("arbitrary", "arbitrary", "arbitrary"),
        name="swa_attn",
    )(sink, q, k, v, kc, vc)


def _merge_kernel(a_ref, b_ref, ga_ref, gb_ref, mga_ref, mgb_ref, x_ref, gate_ref,
                  wba_ref, wbb_ref, wo_ref, fg_ref, o_ref):
    a_in = (a_ref[0].astype(F32) * ga_ref[0].astype(F32)).astype(BF16)
    b_in = (b_ref[0].astype(F32) * gb_ref[0].astype(F32)).astype(BF16)
    y_a = _dot(a_in, wba_ref[...])
    y_b = _dot(b_in, wbb_ref[...])
    mix = mga_ref[0].astype(F32) * y_a + mgb_ref[0].astype(F32) * y_b
    out = _dot(mix.astype(BF16), wo_ref[...])
    xn = x_ref[0] + gate_ref[0] * out
    o_ref[0] = _rms(xn, fg_ref[...]).astype(o_ref.dtype)


def _merge(a, bh, gates, x, mod, wba, wbb, wo, fg, tm):
    b, s, d = x.shape
    wa = a.shape[2]
    const = lambda bi, i: (0, 0)
    row = lambda col: (lambda bi, i: (bi, i, col))
    return pl.pallas_call(
        _merge_kernel,
        grid=(b, s // tm),
        in_specs=[pl.BlockSpec((1, tm, wa), row(0)),
                  pl.BlockSpec((1, tm, wa), row(0)),
                  pl.BlockSpec((1, tm, wa), row(0)),
                  pl.BlockSpec((1, tm, wa), row(1)),
                  pl.BlockSpec((1, tm, d), row(1)),
                  pl.BlockSpec((1, tm, d), row(2)),
                  pl.BlockSpec((1, tm, d), row(0)),
                  pl.BlockSpec((1, 1, d), lambda bi, i: (bi, 0, 2)),
                  pl.BlockSpec((wa, d), const),
                  pl.BlockSpec((wa, d), const),
                  pl.BlockSpec((d, d), const),
                  pl.BlockSpec((1, d), const)],
        out_specs=pl.BlockSpec((1, tm, d), row(0)),
        out_shape=jax.ShapeDtypeStruct((b, s, d), x.dtype),
        compiler_params=_params("arbitrary", "arbitrary"),
        name="merge_out",
    )(a, bh, gates, gates, gates, gates, x, mod, wba, wbb, wo, fg.reshape(1, d))


def _rope_tables(s):
    t = jnp.arange(s, dtype=jnp.int32)
    n = 16
    inv = ROPE_BASE ** (-jnp.arange(n, dtype=F32) / n)
    ang_r = (t // GRID_W).astype(F32)[:, None] * inv[None, :]
    ang_c = (t % GRID_W).astype(F32)[:, None] * inv[None, :]
    cos = jnp.concatenate([jnp.cos(ang_r)] * 2 + [jnp.cos(ang_c)] * 2, axis=1)
    sin = jnp.concatenate([-jnp.sin(ang_r), jnp.sin(ang_r), -jnp.sin(ang_c), jnp.sin(ang_c)], axis=1)
    return jnp.tile(cos, (1, 2)), jnp.tile(sin, (1, 2))


def _dup_heads(w, heads, dim):
    lead = w.shape[:-1]
    w = w.reshape(lead + (heads, 1, dim))
    return jnp.broadcast_to(w, lead + (heads, 2, dim)).reshape(lead + (heads * 2 * dim,))


def kernel(x, c, ctx, c_ctx, w_ada, b_ada, norm_g, w_in, b_in, q_norm_g, kv_norm_g, w_uq, w_ukv, sink,
           w_branch_a, w_branch_b, w_out, final_g):
    bsz, s, d = x.shape
    c_len = ctx.shape[1]
    assert w_ada.shape[0] == 1, "single-layer block"
    widths = (MLA_RANK, MLA_RANK, MLA_ROPE, MLA_HEADS * MLA_V, SWA_HEADS * SWA_DIM,
              SWA_KV_HEADS * SWA_DIM, SWA_KV_HEADS * SWA_DIM, SWA_HEADS * SWA_DIM, d, d)
    assert sum(widths) == w_in.shape[2]
    offs = [0]
    for w_ in widths:
        offs.append(offs[-1] + w_)
    w_in0, b_in0 = w_in[0], b_in[0]
    col = lambda k: (w_in0[:, offs[k]:offs[k + 1]], b_in0[offs[k]:offs[k + 1]])
    (w_cq, b_cq), (w_ckv, b_ckv), (w_kpe, b_kpe), (w_ga, b_ga), (w_qs, b_qs) = (col(k) for k in range(5))
    (w_ks, b_ks), (w_vs, b_vs), (w_gb, b_gb), (w_mga, b_mga), (w_mgb, b_mgb) = (col(k) for k in range(5, 10))

    rows = 8
    c_rows = jnp.concatenate([c, c_ctx[None, :], jnp.zeros((rows - bsz - 1, d), F32)], axis=0)
    mod = _modulation(c_rows, w_ada[0], b_ada[0])
    mod_lat = mod[:bsz].reshape(bsz, 1, 3 * d)
    mod_ctx = jnp.broadcast_to(mod[bsz].reshape(1, 1, 3 * d), (bsz, 1, 3 * d))

    h = _norm_mod(x, norm_g[0], mod_lat, NORM_TM)
    h_c = _norm_mod(ctx, norm_g[0], mod_ctx, c_len)

    pad_pe = LANE - MLA_ROPE
    w_mla = jnp.concatenate([w_cq, w_ckv, w_kpe, jnp.zeros((d, pad_pe), F32)], axis=1).astype(BF16)
    b_mla = jnp.concatenate([b_cq, b_ckv, b_kpe, jnp.zeros((pad_pe,), F32)])
    wuq = jnp.pad(w_uq[0].reshape(MLA_RANK, MLA_HEADS, MLA_QK),
                  ((0, 0), (0, 0), (0, MLA_QK_PAD - MLA_QK))).reshape(MLA_RANK, -1).astype(BF16)
    wukv = w_ukv[0].astype(BF16)
    w_swa = jnp.concatenate([w_qs, _dup_heads(w_ks, SWA_KV_HEADS, SWA_DIM),
                             _dup_heads(w_vs, SWA_KV_HEADS, SWA_DIM)], axis=1).astype(BF16)
    b_swa = jnp.concatenate([b_qs, _dup_heads(b_ks, SWA_KV_HEADS, SWA_DIM),
                             _dup_heads(b_vs, SWA_KV_HEADS, SWA_DIM)])
    w_gates = jnp.concatenate([w_ga, w_gb, w_mga, w_mgb], axis=1).astype(BF16)
    b_gates = jnp.concatenate([b_ga, b_gb, b_mga, b_mgb])

    cos, sin = _rope_tables(s)
    cos_c = jnp.ones((c_len, LANE), F32)
    sin_c = jnp.zeros((c_len, LANE), F32)

    q_mla, k_mla, vt_mla = _mla_proj(h, w_mla, b_mla, q_norm_g[0], kv_norm_g[0], wuq, wukv, cos, sin,
                                     PROJ_TM, MLA_TK)
    _, kc_mla, vtc_mla = _mla_proj(h_c, w_mla, b_mla, q_norm_g[0], kv_norm_g[0], wuq, wukv, cos_c, sin_c,
                                   c_len, c_len)
    q_s, k_s, v_s = _swa_proj(h, w_swa, b_swa, cos, sin, PROJ_TM)
    _, kc_s, vc_s = _swa_proj(h_c, w_swa, b_swa, cos_c, sin_c, c_len)
    gates = _gates(h, w_gates, b_gates, 2 * MLA_HEADS * MLA_V, GATE_TM, GATE_TN)

    a_heads = _mla_attn(q_mla, k_mla, vt_mla, kc_mla, vtc_mla, MLA_TQ, MLA_SUB)
    b_heads = _swa_attn(sink[0], q_s, k_s, v_s, kc_s, vc_s, SWA_TQ)

    return _merge(a_heads, b_heads, gates, x, mod_lat, w_branch_a[0].astype(BF16),
                  w_branch_b[0].astype(BF16), w_out[0].astype(BF16), final_g, MERGE_TM)
```

```python
import functools

import jax
import jax.numpy as jnp
from jax import lax
from jax.experimental import pallas as pl
from jax.experimental.pallas import tpu as pltpu

F32 = jnp.float32
BF16 = jnp.bfloat16

LANE = 128
EPS = 1e-6
NEG = -1e30
ROPE_BASE = 10000.0
GRID_W = 64
LOG2E = 1.4426950408889634
WINDOW = 128

MLA_HEADS = 8
MLA_NOPE = 128
MLA_ROPE = 64
MLA_V = 128
MLA_QK = MLA_NOPE + MLA_ROPE
MLA_QK_PAD = 2 * LANE
MLA_RANK = 512
SWA_HEADS = 16
SWA_KV_HEADS = 4
SWA_GROUP = SWA_HEADS // SWA_KV_HEADS
SWA_DIM = 64

VMEM_LIMIT = 56 * 1024 * 1024

NORM_TM = 512
PROJ_TM = 1024
GATE_TM, GATE_TN = 1024, 2048
MLA_TQ = 1024
MLA_TK = 2048
MLA_SUB = 512
SWA_TQ = 512
MERGE_TM = 512


def _params(*sem):
    return pltpu.CompilerParams(dimension_semantics=sem, vmem_limit_bytes=VMEM_LIMIT)


def _sigmoid(x):
    return 1.0 / (1.0 + jnp.exp(-x))


def _dot(a, b):
    return jnp.dot(a, b, preferred_element_type=F32)


def _dot_nt(a, b):
    return lax.dot_general(a, b, (((1,), (1,)), ((), ())), preferred_element_type=F32)


def _rope_tile(x, cos, sin_signed):
    lane = lax.broadcasted_iota(jnp.int32, x.shape, 1)
    first = (lane % 32) < 16
    partner = jnp.where(first, pltpu.roll(x, LANE - 16, 1), pltpu.roll(x, 16, 1))
    return x * cos + partner * sin_signed


def _split_bf16(a):
    hi = a.astype(BF16)
    lo = (a - hi.astype(F32)).astype(BF16)
    return hi, lo


def _mod_kernel(c_ref, w_ref, b_ref, o_ref):
    c = c_ref[...]
    s = c * _sigmoid(c)
    s_hi, s_lo = _split_bf16(s)
    w_hi, w_lo = _split_bf16(w_ref[0])
    acc = _dot(s_hi, w_hi) + (_dot(s_hi, w_lo) + _dot(s_lo, w_hi))
    o_ref[...] = acc + b_ref[...]


def _modulation(c_rows, w_ada, b_ada):
    rows, d = c_rows.shape
    n = w_ada.shape[2]
    tn = 1536
    return pl.pallas_call(
        _mod_kernel,
        grid=(n // tn,),
        in_specs=[pl.BlockSpec((rows, d), lambda j: (0, 0)),
                  pl.BlockSpec((1, d, tn), lambda j: (0, 0, j)),
                  pl.BlockSpec((1, tn), lambda j: (0, j))],
        out_specs=pl.BlockSpec((rows, tn), lambda j: (0, j)),
        out_shape=jax.ShapeDtypeStruct((rows, n), F32),
        compiler_params=_params("arbitrary"),
        name="adaln_mod",
    )(c_rows, w_ada, b_ada.reshape(1, n))


def _norm_mod_kernel(x_ref, g_ref, shift_ref, scale_ref, h_ref):
    x = x_ref[0]
    y = x * lax.rsqrt(jnp.mean(x * x, axis=-1, keepdims=True) + EPS)
    h = (y * g_ref[...]) * (1.0 + scale_ref[0]) + shift_ref[0]
    h_ref[0] = h.astype(h_ref.dtype)


def _norm_mod(x, g, mod, tm):
    b, s, d = x.shape
    return pl.pallas_call(
        _norm_mod_kernel,
        grid=(b, s // tm),
        in_specs=[pl.BlockSpec((1, tm, d), lambda bi, i: (bi, i, 0)),
                  pl.BlockSpec((1, d), lambda bi, i: (0, 0)),
                  pl.BlockSpec((1, 1, d), lambda bi, i: (bi, 0, 0)),
                  pl.BlockSpec((1, 1, d), lambda bi, i: (bi, 0, 1))],
        out_specs=pl.BlockSpec((1, tm, d), lambda bi, i: (bi, i, 0)),
        out_shape=jax.ShapeDtypeStruct((b, s, d), BF16),
        compiler_params=_params("arbitrary", "arbitrary"),
        name="norm_mod",
    )(x, g.reshape(1, d), mod, mod)


def _gates_kernel(h_ref, w_ref, b_ref, o_ref, *, silu_blocks):
    z = _dot(h_ref[0], w_ref[...]) + b_ref[...]
    sg = _sigmoid(z)
    is_silu = pl.program_id(2) < silu_blocks
    o_ref[0] = (sg * jnp.where(is_silu, z, 1.0)).astype(o_ref.dtype)


def _gates(h, w, bias, silu_cols, tm, tn):
    b, s, d = h.shape
    n = w.shape[1]
    return pl.pallas_call(
        functools.partial(_gates_kernel, silu_blocks=silu_cols // tn),
        grid=(b, s // tm, n // tn),
        in_specs=[pl.BlockSpec((1, tm, d), lambda bi, i, j: (bi, i, 0)),
                  pl.BlockSpec((d, tn), lambda bi, i, j: (0, j)),
                  pl.BlockSpec((1, tn), lambda bi, i, j: (0, j))],
        out_specs=pl.BlockSpec((1, tm, tn), lambda bi, i, j: (bi, i, j)),
        out_shape=jax.ShapeDtypeStruct((b, s, n), BF16),
        compiler_params=_params("arbitrary", "arbitrary", "arbitrary"),
        name="gate_proj",
    )(h, w, bias.reshape(1, n))


def _rms(x, g):
    return (x * lax.rsqrt(jnp.mean(x * x, axis=-1, keepdims=True) + EPS)) * g


def _mla_proj_kernel(h_ref, w_ref, b_ref, qg_ref, kvg_ref, wuq_ref, wukv_ref, cos_ref, sin_ref,
                     q_ref, k_ref, vt_ref, *, q_scale):
    z = _dot(h_ref[0], w_ref[...]) + b_ref[...]
    cos = cos_ref[...]
    sin = sin_ref[...]
    cq = _rms(z[:, :MLA_RANK], qg_ref[...]).astype(BF16)
    ckv = _rms(z[:, MLA_RANK:2 * MLA_RANK], kvg_ref[...]).astype(BF16)
    k_pe = _rope_tile(z[:, 2 * MLA_RANK:], cos, sin).astype(k_ref.dtype)
    q = _dot(cq, wuq_ref[...])
    kv = _dot(ckv, wukv_ref[...])
    for hd in range(MLA_HEADS):
        base = hd * MLA_QK_PAD
        q_rope = _rope_tile(q[:, base + LANE:base + 2 * LANE], cos, sin)
        q_ref[0, base:base + LANE, :] = (q[:, base:base + LANE] * q_scale).T.astype(q_ref.dtype)
        q_ref[0, base + LANE:base + 2 * LANE, :] = (q_rope * q_scale).T.astype(q_ref.dtype)
        k_ref[0, :, base:base + LANE] = kv[:, base:base + LANE].astype(k_ref.dtype)
        k_ref[0, :, base + LANE:base + 2 * LANE] = k_pe
        vt_ref[0, 0, hd * MLA_V:(hd + 1) * MLA_V, :] = kv[:, base + LANE:base + 2 * LANE].T.astype(vt_ref.dtype)


def _mla_proj(h, w, bias, qg, kvg, wuq, wukv, cos, sin, tm, tkc):
    b, s, d = h.shape
    n = w.shape[1]
    nq = MLA_HEADS * MLA_QK_PAD
    nv = MLA_HEADS * MLA_V
    per_chunk = tkc // tm
    const = lambda bi, i: (0, 0)
    return pl.pallas_call(
        functools.partial(_mla_proj_kernel, q_scale=float(MLA_QK) ** -0.5 * LOG2E),
        grid=(b, s // tm),
        in_specs=[pl.BlockSpec((1, tm, d), lambda bi, i: (bi, i, 0)),
                  pl.BlockSpec((d, n), const),
                  pl.BlockSpec((1, n), const),
                  pl.BlockSpec((1, MLA_RANK), const),
                  pl.BlockSpec((1, MLA_RANK), const),
                  pl.BlockSpec((MLA_RANK, nq), const),
                  pl.BlockSpec((MLA_RANK, nq), const),
                  pl.BlockSpec((tm, LANE), lambda bi, i: (i, 0)),
                  pl.BlockSpec((tm, LANE), lambda bi, i: (i, 0))],
        out_specs=[pl.BlockSpec((1, nq, tm), lambda bi, i: (bi, 0, i)),
                   pl.BlockSpec((1, tm, nq), lambda bi, i: (bi, i, 0)),
                   pl.BlockSpec((1, 1, nv, tm), lambda bi, i: (bi, i // per_chunk, 0, i % per_chunk))],
        out_shape=[jax.ShapeDtypeStruct((b, nq, s), BF16),
                   jax.ShapeDtypeStruct((b, s, nq), BF16),
                   jax.ShapeDtypeStruct((b, s // tkc, nv, tkc), BF16)],
        compiler_params=_params("arbitrary", "arbitrary"),
        name="mla_proj",
    )(h, w, bias.reshape(1, n), qg.reshape(1, -1), kvg.reshape(1, -1), wuq, wukv, cos, sin)


def _swa_proj_kernel(h_ref, w_ref, b_ref, cos_ref, sin_ref, q_ref, k_ref, v_ref, *, nq, nk, q_scale):
    z = _dot(h_ref[0], w_ref[...]) + b_ref[...]
    cos = cos_ref[...]
    sin = sin_ref[...]
    for t in range(nq // LANE):
        q_t = _rope_tile(z[:, t * LANE:(t + 1) * LANE], cos, sin) * q_scale
        q_ref[0, :, t * LANE:(t + 1) * LANE] = q_t.astype(q_ref.dtype)
    for t in range(nk // LANE):
        k_t = _rope_tile(z[:, nq + t * LANE:nq + (t + 1) * LANE], cos, sin)
        k_ref[0, :, t * LANE:(t + 1) * LANE] = k_t.astype(k_ref.dtype)
    v_ref[0] = z[:, nq + nk:].astype(v_ref.dtype)


def _swa_proj(h, w, bias, cos, sin, tm):
    b, s, d = h.shape
    n = w.shape[1]
    nq = SWA_HEADS * SWA_DIM
    nk = SWA_KV_HEADS * LANE
    const = lambda bi, i: (0, 0)
    return pl.pallas_call(
        functools.partial(_swa_proj_kernel, nq=nq, nk=nk, q_scale=float(SWA_DIM) ** -0.5 * LOG2E),
        grid=(b, s // tm),
        in_specs=[pl.BlockSpec((1, tm, d), lambda bi, i: (bi, i, 0)),
                  pl.BlockSpec((d, n), const),
                  pl.BlockSpec((1, n), const),
                  pl.BlockSpec((tm, LANE), lambda bi, i: (i, 0)),
                  pl.BlockSpec((tm, LANE), lambda bi, i: (i, 0))],
        out_specs=[pl.BlockSpec((1, tm, nq), lambda bi, i: (bi, i, 0)),
                   pl.BlockSpec((1, tm, nk), lambda bi, i: (bi, i, 0)),
                   pl.BlockSpec((1, tm, nk), lambda bi, i: (bi, i, 0))],
        out_shape=[jax.ShapeDtypeStruct((b, s, nq), BF16),
                   jax.ShapeDtypeStruct((b, s, nk), BF16),
                   jax.ShapeDtypeStruct((b, s, nk), BF16)],
        compiler_params=_params("arbitrary", "arbitrary"),
        name="swa_proj",
    )(h, w, bias.reshape(1, n), cos, sin)


def _mla_attn_kernel(q_ref, qn_ref, k_ref, vt_ref, kc_ref, vtc_ref, o_ref, s_a, s_b, acc_ref, cmax_ref,
                     *, tk, sub):
    tq = q_ref.shape[2]
    n_lat = k_ref.shape[1] // tk
    c = kc_ref.shape[1]
    s_bufs = (s_a, s_b)

    def pieces(n, with_ctx=False):
        out = []
        for lo in range(0, tk, sub):
            hi = min(lo + sub, tk)
            out.append((slice(lo, hi),
                        lambda lo=lo, hi=hi: k_ref[0, pl.ds(pl.multiple_of(n * tk, tk) + lo, hi - lo), :],
                        lambda lo=lo, hi=hi: vt_ref[0, n, :, lo:hi]))
        if with_ctx:
            out.append((slice(tk, tk + c), lambda: kc_ref[0], lambda: vtc_ref[0, 0]))
        return out

    def fused_step(m, l, score=None, soft=None):
        if soft is not None:
            m_new = jnp.maximum(m, soft[2])
            alpha = jnp.exp2(m - m_new)
        cmax = lsum = pv = None
        n_score = len(score[0]) if score is not None else 0
        n_soft = len(soft[0]) if soft is not None else 0
        for j in range(max(n_score, n_soft)):
            if j < n_score:
                rows, k_load, _ = score[0][j]
                s = _dot(k_load(), score[2][0])
                score[1][rows, :] = s
                piece_max = jnp.max(s, axis=0, keepdims=True)
                cmax = piece_max if cmax is None else jnp.maximum(cmax, piece_max)
            if j < n_soft:
                rows, _, vt_load = soft[0][j]
                p = jnp.exp2(soft[1][rows, :] - m_new)
                piece_sum = jnp.sum(p, axis=0, keepdims=True)
                lsum = piece_sum if lsum is None else lsum + piece_sum
                d = _dot(vt_load(), p.astype(BF16))
                pv = d if pv is None else pv + d
        if soft is not None:
            acc_ref[...] = alpha * acc_ref[...] + pv
            return m_new, alpha * l + lsum, cmax
        return m, l, cmax

    @pl.when(pl.program_id(2) == 0)
    def _():
        cmax_ref[...] = fused_step(None, None, score=(pieces(0), s_bufs[0], q_ref))[2]

    acc_ref[...] = jnp.zeros_like(acc_ref)
    m = jnp.full((1, tq), -jnp.inf, F32)
    l = jnp.zeros((1, tq), F32)
    cmax = cmax_ref[...]

    def step(n, u, carry, last=False):
        m, l, cmax = carry
        return fused_step(m, l, score=(pieces(n, last), s_bufs[u], q_ref),
                          soft=(pieces(n - 1), s_bufs[1 - u], cmax))

    def pair(t, carry):
        return step(2 * t + 2, 0, step(2 * t + 1, 1, carry))

    carry = lax.fori_loop(0, n_lat // 2 - 1, pair, (m, l, cmax))
    m, l, cmax = step(n_lat - 1, 1, carry, last=True)
    m, l, cmax_next = fused_step(m, l, score=(pieces(0), s_bufs[0], qn_ref),
                                 soft=(pieces(n_lat - 1, True), s_bufs[1], cmax))
    cmax_ref[...] = cmax_next
    o_ref[0] = (acc_ref[...] / l).T.astype(o_ref.dtype)


def _mla_attn(qt, k, vt, kc, vtc, tq, sub):
    b, _, s = qt.shape
    c = kc.shape[1]
    n_chunks, _, tk = vt.shape[1:]
    n_q = s // tq
    assert n_chunks % 2 == 0 and n_chunks >= 2
    return pl.pallas_call(
        functools.partial(_mla_attn_kernel, tk=tk, sub=sub),
        grid=(b, MLA_HEADS, n_q),
        in_specs=[pl.BlockSpec((1, MLA_QK_PAD, tq), lambda bi, hd, i: (bi, hd, i)),
                  pl.BlockSpec((1, MLA_QK_PAD, tq), lambda bi, hd, i: (bi, hd, jnp.minimum(i + 1, n_q - 1))),
                  pl.BlockSpec((1, s, MLA_QK_PAD), lambda bi, hd, i: (bi, 0, hd)),
                  pl.BlockSpec((1, n_chunks, MLA_V, tk), lambda bi, hd, i: (bi, 0, hd, 0)),
                  pl.BlockSpec((1, c, MLA_QK_PAD), lambda bi, hd, i: (bi, 0, hd)),
                  pl.BlockSpec((1, 1, MLA_V, c), lambda bi, hd, i: (bi, 0, hd, 0))],
        out_specs=pl.BlockSpec((1, tq, MLA_V), lambda bi, hd, i: (bi, i, hd)),
        out_shape=jax.ShapeDtypeStruct((b, s, MLA_HEADS * MLA_V), BF16),
        scratch_shapes=[pltpu.VMEM((tk + c, tq), F32), pltpu.VMEM((tk + c, tq), F32),
                        pltpu.VMEM((MLA_V, tq), F32), pltpu.VMEM((1, tq), F32)],
        compiler_params=_params("arbitrary", "arbitrary", "arbitrary"),
        name="mla_attn",
    )(qt, qt, k, vt, kc, vtc)


def _swa_attn_kernel(sink_ref, q_ref, k_ref, v_ref, kc_ref, vc_ref, o_ref, *, tq):
    kvh = pl.program_id(1)
    i = pl.program_id(2)
    s_len = k_ref.shape[1]
    blk = LANE
    span = blk + 2 * WINDOW
    rows = SWA_GROUP * blk
    kc = kc_ref[0]
    vc = vc_ref[0]
    low_half = lax.broadcasted_iota(jnp.int32, (blk, LANE), 1) < SWA_DIM
    row_id = lax.broadcasted_iota(jnp.int32, (rows, 1), 0)
    head_of_row = row_id // blk
    sink = jnp.zeros((rows, 1), F32)
    for g in range(SWA_GROUP):
        sink = jnp.where(head_of_row == g, sink_ref[kvh * SWA_GROUP + g] * LOG2E, sink)
    for r in range(tq // blk):
        q0 = i * tq + r * blk
        start = pl.multiple_of(jnp.clip(q0 - WINDOW, 0, s_len - span), LANE)
        kw = k_ref[0, pl.ds(start, span), :]
        vw = v_ref[0, pl.ds(start, span), :]
        qpos = q0 + lax.broadcasted_iota(jnp.int32, (rows, span), 0) % blk
        kpos = start + lax.broadcasted_iota(jnp.int32, (rows, span), 1)
        band = jnp.abs(qpos - kpos) <= WINDOW
        q_rows = []
        for g in range(SWA_GROUP):
            q_t = q_ref[0, r * blk:(r + 1) * blk, (g // 2) * LANE:(g // 2 + 1) * LANE]
            keep = low_half if g % 2 == 0 else jnp.logical_not(low_half)
            q_rows.append(jnp.where(keep, q_t, jnp.zeros_like(q_t)))
        q4 = jnp.concatenate(q_rows, axis=0)
        s_w = jnp.where(band, _dot_nt(q4, kw), NEG)
        s_c = _dot_nt(q4, kc)
        m = jnp.maximum(jnp.maximum(jnp.max(s_w, axis=1, keepdims=True),
                                    jnp.max(s_c, axis=1, keepdims=True)), sink)
        p_w = jnp.exp2(s_w - m)
        p_c = jnp.exp2(s_c - m)
        l = jnp.sum(p_w, axis=1, keepdims=True) + jnp.sum(p_c, axis=1, keepdims=True) + jnp.exp2(sink - m)
        o = (_dot(p_w.astype(vw.dtype), vw) + _dot(p_c.astype(vc.dtype), vc)) / l
        for t in range(SWA_GROUP // 2):
            pair = jnp.where(low_half, o[2 * t * blk:(2 * t + 1) * blk], o[(2 * t + 1) * blk:(2 * t + 2) * blk])
            o_ref[0, r * blk:(r + 1) * blk, t * LANE:(t + 1) * LANE] = pair.astype(o_ref.dtype)


def _swa_attn(sink, q, k, v, kc, vc, tq):
    b, s, _ = q.shape
    c = kc.shape[1]
    gw = SWA_GROUP * SWA_DIM
    return pl.pallas_call(
        functools.partial(_swa_attn_kernel, tq=tq),
        grid=(b, SWA_KV_HEADS, s // tq),
        in_specs=[pl.BlockSpec(memory_space=pltpu.SMEM),
                  pl.BlockSpec((1, tq, gw), lambda bi, kh, i: (bi, i, kh)),
                  pl.BlockSpec((1, s, LANE), lambda bi, kh, i: (bi, 0, kh)),
                  pl.BlockSpec((1, s, LANE), lambda bi, kh, i: (bi, 0, kh)),
                  pl.BlockSpec((1, c, LANE), lambda bi, kh, i: (bi, 0, kh)),
                  pl.BlockSpec((1, c, LANE), lambda bi, kh, i: (bi, 0, kh))],
        out_specs=pl.BlockSpec((1, tq, gw), lambda bi, kh, i: (bi, i, kh)),
        out_shape=jax.ShapeDtypeStruct((b, s, SWA_HEADS * SWA_DIM), BF16),
        compiler_params=_params("arbitrary", "arbitrary", "arbitrary"),
        name="swa_attn",
    )(sink, q, k, v, kc, vc)


def _merge_kernel(a_ref, b_ref, ga_ref, gb_ref, mga_ref, mgb_ref, x_ref, gate_ref,
                  wba_ref, wbb_ref, wo_ref, fg_ref, o_ref):
    a_in = (a_ref[0].astype(F32) * ga_ref[0].astype(F32)).astype(BF16)
    b_in = (b_ref[0].astype(F32) * gb_ref[0].astype(F32)).astype(BF16)
    y_a = _dot(a_in, wba_ref[...])
    y_b = _dot(b_in, wbb_ref[...])
    mix = mga_ref[0].astype(F32) * y_a + mgb_ref[0].astype(F32) * y_b
    out = _dot(mix.astype(BF16), wo_ref[...])
    xn = x_ref[0] + gate_ref[0] * out
    o_ref[0] = _rms(xn, fg_ref[...]).astype(o_ref.dtype)


def _merge(a, bh, gates, x, mod, wba, wbb, wo, fg, tm):
    b, s, d = x.shape
    wa = a.shape[2]
    const = lambda bi, i: (0, 0)
    row = lambda col: (lambda bi, i: (bi, i, col))
    return pl.pallas_call(
        _merge_kernel,
        grid=(b, s // tm),
        in_specs=[pl.BlockSpec((1, tm, wa), row(0)),
                  pl.BlockSpec((1, tm, wa), row(0)),
                  pl.BlockSpec((1, tm, wa), row(0)),
                  pl.BlockSpec((1, tm, wa), row(1)),
                  pl.BlockSpec((1, tm, d), row(1)),
                  pl.BlockSpec((1, tm, d), row(2)),
                  pl.BlockSpec((1, tm, d), row(0)),
                  pl.BlockSpec((1, 1, d), lambda bi, i: (bi, 0, 2)),
                  pl.BlockSpec((wa, d), const, pipeline_mode=pl.Buffered(1)),
                  pl.BlockSpec((wa, d), const, pipeline_mode=pl.Buffered(1)),
                  pl.BlockSpec((d, d), const, pipeline_mode=pl.Buffered(1)),
                  pl.BlockSpec((1, d), const)],
        out_specs=pl.BlockSpec((1, tm, d), row(0)),
        out_shape=jax.ShapeDtypeStruct((b, s, d), x.dtype),
        compiler_params=_params("arbitrary", "arbitrary"),
        name="merge_out",
    )(a, bh, gates, gates, gates, gates, x, mod, wba, wbb, wo, fg.reshape(1, d))


def _rope_tables(s):
    t = jnp.arange(s, dtype=jnp.int32)
    n = 16
    inv = ROPE_BASE ** (-jnp.arange(n, dtype=F32) / n)
    ang_r = (t // GRID_W).astype(F32)[:, None] * inv[None, :]
    ang_c = (t % GRID_W).astype(F32)[:, None] * inv[None, :]
    cos = jnp.concatenate([jnp.cos(ang_r)] * 2 + [jnp.cos(ang_c)] * 2, axis=1)
    sin = jnp.concatenate([-jnp.sin(ang_r), jnp.sin(ang_r), -jnp.sin(ang_c), jnp.sin(ang_c)], axis=1)
    return jnp.tile(cos, (1, 2)), jnp.tile(sin, (1, 2))


def _dup_heads(w, heads, dim):
    lead = w.shape[:-1]
    w = w.reshape(lead + (heads, 1, dim))
    return jnp.broadcast_to(w, lead + (heads, 2, dim)).reshape(lead + (heads * 2 * dim,))


def kernel(x, c, ctx, c_ctx, w_ada, b_ada, norm_g, w_in, b_in, q_norm_g, kv_norm_g, w_uq, w_ukv, sink,
           w_branch_a, w_branch_b, w_out, final_g):
    bsz, s, d = x.shape
    c_len = ctx.shape[1]
    assert w_ada.shape[0] == 1, "single-layer block"
    widths = (MLA_RANK, MLA_RANK, MLA_ROPE, MLA_HEADS * MLA_V, SWA_HEADS * SWA_DIM,
              SWA_KV_HEADS * SWA_DIM, SWA_KV_HEADS * SWA_DIM, SWA_HEADS * SWA_DIM, d, d)
    assert sum(widths) == w_in.shape[2]
    offs = [0]
    for w_ in widths:
        offs.append(offs[-1] + w_)
    w_in0, b_in0 = w_in[0], b_in[0]
    col = lambda k: (w_in0[:, offs[k]:offs[k + 1]], b_in0[offs[k]:offs[k + 1]])
    (w_cq, b_cq), (w_ckv, b_ckv), (w_kpe, b_kpe), (w_ga, b_ga), (w_qs, b_qs) = (col(k) for k in range(5))
    (w_ks, b_ks), (w_vs, b_vs), (w_gb, b_gb), (w_mga, b_mga), (w_mgb, b_mgb) = (col(k) for k in range(5, 10))

    rows = 8
    c_rows = jnp.concatenate([c, c_ctx[None, :], jnp.zeros((rows - bsz - 1, d), F32)], axis=0)
    mod = _modulation(c_rows, w_ada, b_ada[0])
    mod_lat = mod[:bsz].reshape(bsz, 1, 3 * d)
    mod_ctx = jnp.broadcast_to(mod[bsz].reshape(1, 1, 3 * d), (bsz, 1, 3 * d))

    h = _norm_mod(x, norm_g[0], mod_lat, NORM_TM)
    h_c = _norm_mod(ctx, norm_g[0], mod_ctx, c_len)

    pad_pe = LANE - MLA_ROPE
    w_mla = jnp.concatenate([w_cq, w_ckv, w_kpe, jnp.zeros((d, pad_pe), F32)], axis=1).astype(BF16)
    b_mla = jnp.concatenate([b_cq, b_ckv, b_kpe, jnp.zeros((pad_pe,), F32)])
    wuq = jnp.pad(w_uq[0].reshape(MLA_RANK, MLA_HEADS, MLA_QK),
                  ((0, 0), (0, 0), (0, MLA_QK_PAD - MLA_QK))).reshape(MLA_RANK, -1).astype(BF16)
    wukv = w_ukv[0].astype(BF16)
    w_swa = jnp.concatenate([w_qs, _dup_heads(w_ks, SWA_KV_HEADS, SWA_DIM),
                             _dup_heads(w_vs, SWA_KV_HEADS, SWA_DIM)], axis=1).astype(BF16)
    b_swa = jnp.concatenate([b_qs, _dup_heads(b_ks, SWA_KV_HEADS, SWA_DIM),
                             _dup_heads(b_vs, SWA_KV_HEADS, SWA_DIM)])
    w_gates = jnp.concatenate([w_ga, w_gb, w_mga, w_mgb], axis=1).astype(BF16)
    b_gates = jnp.concatenate([b_ga, b_gb, b_mga, b_mgb])

    cos, sin = _rope_tables(s)
    cos_c = jnp.ones((c_len, LANE), F32)
    sin_c = jnp.zeros((c_len, LANE), F32)

    q_mla, k_mla, vt_mla = _mla_proj(h, w_mla, b_mla, q_norm_g[0], kv_norm_g[0], wuq, wukv, cos, sin,
                                     PROJ_TM, MLA_TK)
    _, kc_mla, vtc_mla = _mla_proj(h_c, w_mla, b_mla, q_norm_g[0], kv_norm_g[0], wuq, wukv, cos_c, sin_c,
                                   c_len, c_len)
    q_s, k_s, v_s = _swa_proj(h, w_swa, b_swa, cos, sin, PROJ_TM)
    _, kc_s, vc_s = _swa_proj(h_c, w_swa, b_swa, cos_c, sin_c, c_len)
    gates = _gates(h, w_gates, b_gates, 2 * MLA_HEADS * MLA_V, GATE_TM, GATE_TN)

    a_heads = _mla_attn(q_mla, k_mla, vt_mla, kc_mla, vtc_mla, MLA_TQ, MLA_SUB)
    b_heads = _swa_attn(sink[0], q_s, k_s, v_s, kc_s, vc_s, SWA_TQ)

    return _merge(a_heads, b_heads, gates, x, mod_lat, w_branch_a[0].astype(BF16),
                  w_branch_b[0].astype(BF16), w_out[0].astype(BF16), final_g, MERGE_TM)
```

```python
import functools

import jax
import jax.numpy as jnp
from jax import lax
from jax.experimental import pallas as pl
from jax.experimental.pallas import tpu as pltpu

F32 = jnp.float32
BF16 = jnp.bfloat16

LANE = 128
EPS = 1e-6
NEG = -1e30
ROPE_BASE = 10000.0
GRID_W = 64
LOG2E = 1.4426950408889634
WINDOW = 128

MLA_HEADS = 8
MLA_NOPE = 128
MLA_ROPE = 64
MLA_V = 128
MLA_QK = MLA_NOPE + MLA_ROPE
MLA_QK_PAD = 2 * LANE
MLA_RANK = 512
SWA_HEADS = 16
SWA_KV_HEADS = 4
SWA_GROUP = SWA_HEADS // SWA_KV_HEADS
SWA_DIM = 64

VMEM_LIMIT = 56 * 1024 * 1024

NORM_TM = 512
PROJ_TM = 1024
GATE_TM, GATE_TN = 1024, 2048
MLA_TQ = 1024
MLA_TK = 2048
MLA_SUB = 512
SWA_TQ = 1024
MERGE_TM = 512


def _params(*sem):
    return pltpu.CompilerParams(dimension_semantics=sem, vmem_limit_bytes=VMEM_LIMIT)


def _sigmoid(x):
    return 1.0 / (1.0 + jnp.exp(-x))


def _dot(a, b):
    return jnp.dot(a, b, preferred_element_type=F32)


def _dot_nt(a, b):
    return lax.dot_general(a, b, (((1,), (1,)), ((), ())), preferred_element_type=F32)


def _rope_tile(x, cos, sin_signed):
    lane = lax.broadcasted_iota(jnp.int32, x.shape, 1)
    first = (lane % 32) < 16
    partner = jnp.where(first, pltpu.roll(x, LANE - 16, 1), pltpu.roll(x, 16, 1))
    return x * cos + partner * sin_signed


def _split_bf16(a):
    hi = a.astype(BF16)
    lo = (a - hi.astype(F32)).astype(BF16)
    return hi, lo


def _mod_kernel(c_ref, w_ref, b_ref, o_ref):
    c = c_ref[...]
    s = c * _sigmoid(c)
    s_hi, s_lo = _split_bf16(s)
    w_hi, w_lo = _split_bf16(w_ref[0])
    acc = _dot(s_hi, w_hi) + (_dot(s_hi, w_lo) + _dot(s_lo, w_hi))
    o_ref[...] = acc + b_ref[...]


def _modulation(c_rows, w_ada, b_ada):
    rows, d = c_rows.shape
    n = w_ada.shape[2]
    tn = 1536
    return pl.pallas_call(
        _mod_kernel,
        grid=(n // tn,),
        in_specs=[pl.BlockSpec((rows, d), lambda j: (0, 0)),
                  pl.BlockSpec((1, d, tn), lambda j: (0, 0, j)),
                  pl.BlockSpec((1, tn), lambda j: (0, j))],
        out_specs=pl.BlockSpec((rows, tn), lambda j: (0, j)),
        out_shape=jax.ShapeDtypeStruct((rows, n), F32),
        compiler_params=_params("arbitrary"),
        name="adaln_mod",
    )(c_rows, w_ada, b_ada.reshape(1, n))


def _norm_mod_kernel(x_ref, g_ref, shift_ref, scale_ref, h_ref):
    x = x_ref[0]
    y = x * lax.rsqrt(jnp.mean(x * x, axis=-1, keepdims=True) + EPS)
    h = (y * g_ref[...]) * (1.0 + scale_ref[0]) + shift_ref[0]
    h_ref[0] = h.astype(h_ref.dtype)


def _norm_mod(x, g, mod, tm):
    b, s, d = x.shape
    return pl.pallas_call(
        _norm_mod_kernel,
        grid=(b, s // tm),
        in_specs=[pl.BlockSpec((1, tm, d), lambda bi, i: (bi, i, 0)),
                  pl.BlockSpec((1, d), lambda bi, i: (0, 0)),
                  pl.BlockSpec((1, 1, d), lambda bi, i: (bi, 0, 0)),
                  pl.BlockSpec((1, 1, d), lambda bi, i: (bi, 0, 1))],
        out_specs=pl.BlockSpec((1, tm, d), lambda bi, i: (bi, i, 0)),
        out_shape=jax.ShapeDtypeStruct((b, s, d), BF16),
        compiler_params=_params("arbitrary", "arbitrary"),
        name="norm_mod",
    )(x, g.reshape(1, d), mod, mod)


def _gates_kernel(h_ref, w_ref, b_ref, o_ref, *, silu_blocks):
    z = _dot(h_ref[0], w_ref[...]) + b_ref[...]
    sg = _sigmoid(z)
    is_silu = pl.program_id(2) < silu_blocks
    o_ref[0] = (sg * jnp.where(is_silu, z, 1.0)).astype(o_ref.dtype)


def _gates(h, w, bias, silu_cols, tm, tn):
    b, s, d = h.shape
    n = w.shape[1]
    return pl.pallas_call(
        functools.partial(_gates_kernel, silu_blocks=silu_cols // tn),
        grid=(b, s // tm, n // tn),
        in_specs=[pl.BlockSpec((1, tm, d), lambda bi, i, j: (bi, i, 0)),
                  pl.BlockSpec((d, tn), lambda bi, i, j: (0, j)),
                  pl.BlockSpec((1, tn), lambda bi, i, j: (0, j))],
        out_specs=pl.BlockSpec((1, tm, tn), lambda bi, i, j: (bi, i, j)),
        out_shape=jax.ShapeDtypeStruct((b, s, n), BF16),
        compiler_params=_params("arbitrary", "arbitrary", "arbitrary"),
        name="gate_proj",
    )(h, w, bias.reshape(1, n))


def _rms(x, g):
    return (x * lax.rsqrt(jnp.mean(x * x, axis=-1, keepdims=True) + EPS)) * g


def _mla_proj_kernel(h_ref, w_ref, b_ref, qg_ref, kvg_ref, wuq_ref, wukv_ref, cos_ref, sin_ref,
                     q_ref, k_ref, vt_ref, *, q_scale):
    z = _dot(h_ref[0], w_ref[...]) + b_ref[...]
    cos = cos_ref[...]
    sin = sin_ref[...]
    cq = _rms(z[:, :MLA_RANK], qg_ref[...]).astype(BF16)
    ckv = _rms(z[:, MLA_RANK:2 * MLA_RANK], kvg_ref[...]).astype(BF16)
    k_pe = _rope_tile(z[:, 2 * MLA_RANK:], cos, sin).astype(k_ref.dtype)
    q = _dot(cq, wuq_ref[...])
    kv = _dot(ckv, wukv_ref[...])
    for hd in range(MLA_HEADS):
        base = hd * MLA_QK_PAD
        q_rope = _rope_tile(q[:, base + LANE:base + 2 * LANE], cos, sin)
        q_ref[0, base:base + LANE, :] = (q[:, base:base + LANE] * q_scale).T.astype(q_ref.dtype)
        q_ref[0, base + LANE:base + 2 * LANE, :] = (q_rope * q_scale).T.astype(q_ref.dtype)
        k_ref[0, :, base:base + LANE] = kv[:, base:base + LANE].astype(k_ref.dtype)
        k_ref[0, :, base + LANE:base + 2 * LANE] = k_pe
        vt_ref[0, 0, hd * MLA_V:(hd + 1) * MLA_V, :] = kv[:, base + LANE:base + 2 * LANE].T.astype(vt_ref.dtype)


def _mla_proj(h, w, bias, qg, kvg, wuq, wukv, cos, sin, tm, tkc):
    b, s, d = h.shape
    n = w.shape[1]
    nq = MLA_HEADS * MLA_QK_PAD
    nv = MLA_HEADS * MLA_V
    per_chunk = tkc // tm
    const = lambda bi, i: (0, 0)
    return pl.pallas_call(
        functools.partial(_mla_proj_kernel, q_scale=float(MLA_QK) ** -0.5 * LOG2E),
        grid=(b, s // tm),
        in_specs=[pl.BlockSpec((1, tm, d), lambda bi, i: (bi, i, 0)),
                  pl.BlockSpec((d, n), const),
                  pl.BlockSpec((1, n), const),
                  pl.BlockSpec((1, MLA_RANK), const),
                  pl.BlockSpec((1, MLA_RANK), const),
                  pl.BlockSpec((MLA_RANK, nq), const),
                  pl.BlockSpec((MLA_RANK, nq), const),
                  pl.BlockSpec((tm, LANE), lambda bi, i: (i, 0)),
                  pl.BlockSpec((tm, LANE), lambda bi, i: (i, 0))],
        out_specs=[pl.BlockSpec((1, nq, tm), lambda bi, i: (bi, 0, i)),
                   pl.BlockSpec((1, tm, nq), lambda bi, i: (bi, i, 0)),
                   pl.BlockSpec((1, 1, nv, tm), lambda bi, i: (bi, i // per_chunk, 0, i % per_chunk))],
        out_shape=[jax.ShapeDtypeStruct((b, nq, s), BF16),
                   jax.ShapeDtypeStruct((b, s, nq), BF16),
                   jax.ShapeDtypeStruct((b, s // tkc, nv, tkc), BF16)],
        compiler_params=_params("arbitrary", "arbitrary"),
        name="mla_proj",
    )(h, w, bias.reshape(1, n), qg.reshape(1, -1), kvg.reshape(1, -1), wuq, wukv, cos, sin)


def _swa_proj_kernel(h_ref, w_ref, b_ref, cos_ref, sin_ref, q_ref, k_ref, v_ref, *, nq, nk, q_scale):
    z = _dot(h_ref[0], w_ref[...]) + b_ref[...]
    cos = cos_ref[...]
    sin = sin_ref[...]
    for t in range(nq // LANE):
        q_t = _rope_tile(z[:, t * LANE:(t + 1) * LANE], cos, sin) * q_scale
        q_ref[0, :, t * LANE:(t + 1) * LANE] = q_t.astype(q_ref.dtype)
    for t in range(nk // LANE):
        k_t = _rope_tile(z[:, nq + t * LANE:nq + (t + 1) * LANE], cos, sin)
        k_ref[0, :, t * LANE:(t + 1) * LANE] = k_t.astype(k_ref.dtype)
    v_ref[0] = z[:, nq + nk:].astype(v_ref.dtype)


def _swa_proj(h, w, bias, cos, sin, tm):
    b, s, d = h.shape
    n = w.shape[1]
    nq = SWA_HEADS * SWA_DIM
    nk = SWA_KV_HEADS * LANE
    const = lambda bi, i: (0, 0)
    return pl.pallas_call(
        functools.partial(_swa_proj_kernel, nq=nq, nk=nk, q_scale=float(SWA_DIM) ** -0.5 * LOG2E),
        grid=(b, s // tm),
        in_specs=[pl.BlockSpec((1, tm, d), lambda bi, i: (bi, i, 0)),
                  pl.BlockSpec((d, n), const),
                  pl.BlockSpec((1, n), const),
                  pl.BlockSpec((tm, LANE), lambda bi, i: (i, 0)),
                  pl.BlockSpec((tm, LANE), lambda bi, i: (i, 0))],
        out_specs=[pl.BlockSpec((1, tm, nq), lambda bi, i: (bi, i, 0)),
                   pl.BlockSpec((1, tm, nk), lambda bi, i: (bi, i, 0)),
                   pl.BlockSpec((1, tm, nk), lambda bi, i: (bi, i, 0))],
        out_shape=[jax.ShapeDtypeStruct((b, s, nq), BF16),
                   jax.ShapeDtypeStruct((b, s, nk), BF16),
                   jax.ShapeDtypeStruct((b, s, nk), BF16)],
        compiler_params=_params("arbitrary", "arbitrary"),
        name="swa_proj",
    )(h, w, bias.reshape(1, n), cos, sin)


def _mla_attn_kernel(q_ref, qn_ref, k_ref, vt_ref, kc_ref, vtc_ref, o_ref, s_a, s_b, acc_ref, cmax_ref,
                     *, tk, sub):
    tq = q_ref.shape[2]
    n_lat = k_ref.shape[1] // tk
    c = kc_ref.shape[1]
    s_bufs = (s_a, s_b)

    def pieces(n, with_ctx=False):
        out = []
        for lo in range(0, tk, sub):
            hi = min(lo + sub, tk)
            out.append((slice(lo, hi),
                        lambda lo=lo, hi=hi: k_ref[0, pl.ds(pl.multiple_of(n * tk, tk) + lo, hi - lo), :],
                        lambda lo=lo, hi=hi: vt_ref[0, n, :, lo:hi]))
        if with_ctx:
            out.append((slice(tk, tk + c), lambda: kc_ref[0], lambda: vtc_ref[0, 0]))
        return out

    def fused_step(m, l, score=None, soft=None):
        if soft is not None:
            m_new = jnp.maximum(m, soft[2])
            alpha = jnp.exp2(m - m_new)
        cmax = lsum = pv = None
        n_score = len(score[0]) if score is not None else 0
        n_soft = len(soft[0]) if soft is not None else 0
        for j in range(max(n_score, n_soft)):
            if j < n_score:
                rows, k_load, _ = score[0][j]
                s = _dot(k_load(), score[2][0])
                score[1][rows, :] = s
                piece_max = jnp.max(s, axis=0, keepdims=True)
                cmax = piece_max if cmax is None else jnp.maximum(cmax, piece_max)
            if j < n_soft:
                rows, _, vt_load = soft[0][j]
                p = jnp.exp2(soft[1][rows, :] - m_new)
                piece_sum = jnp.sum(p, axis=0, keepdims=True)
                lsum = piece_sum if lsum is None else lsum + piece_sum
                d = _dot(vt_load(), p.astype(BF16))
                pv = d if pv is None else pv + d
        if soft is not None:
            acc_ref[...] = alpha * acc_ref[...] + pv
            return m_new, alpha * l + lsum, cmax
        return m, l, cmax

    @pl.when(pl.program_id(2) == 0)
    def _():
        cmax_ref[...] = fused_step(None, None, score=(pieces(0), s_bufs[0], q_ref))[2]

    acc_ref[...] = jnp.zeros_like(acc_ref)
    m = jnp.full((1, tq), -jnp.inf, F32)
    l = jnp.zeros((1, tq), F32)
    cmax = cmax_ref[...]

    def step(n, u, carry, last=False):
        m, l, cmax = carry
        return fused_step(m, l, score=(pieces(n, last), s_bufs[u], q_ref),
                          soft=(pieces(n - 1), s_bufs[1 - u], cmax))

    def pair(t, carry):
        return step(2 * t + 2, 0, step(2 * t + 1, 1, carry))

    carry = lax.fori_loop(0, n_lat // 2 - 1, pair, (m, l, cmax))
    m, l, cmax = step(n_lat - 1, 1, carry, last=True)
    m, l, cmax_next = fused_step(m, l, score=(pieces(0), s_bufs[0], qn_ref),
                                 soft=(pieces(n_lat - 1, True), s_bufs[1], cmax))
    cmax_ref[...] = cmax_next
    o_ref[0] = (acc_ref[...] / l).T.astype(o_ref.dtype)


def _mla_attn(qt, k, vt, kc, vtc, tq, sub):
    b, _, s = qt.shape
    c = kc.shape[1]
    n_chunks, _, tk = vt.shape[1:]
    n_q = s // tq
    assert n_chunks % 2 == 0 and n_chunks >= 2
    return pl.pallas_call(
        functools.partial(_mla_attn_kernel, tk=tk, sub=sub),
        grid=(b, MLA_HEADS, n_q),
        in_specs=[pl.BlockSpec((1, MLA_QK_PAD, tq), lambda bi, hd, i: (bi, hd, i)),
                  pl.BlockSpec((1, MLA_QK_PAD, tq), lambda bi, hd, i: (bi, hd, jnp.minimum(i + 1, n_q - 1))),
                  pl.BlockSpec((1, s, MLA_QK_PAD), lambda bi, hd, i: (bi, 0, hd)),
                  pl.BlockSpec((1, n_chunks, MLA_V, tk), lambda bi, hd, i: (bi, 0, hd, 0)),
                  pl.BlockSpec((1, c, MLA_QK_PAD), lambda bi, hd, i: (bi, 0, hd)),
                  pl.BlockSpec((1, 1, MLA_V, c), lambda bi, hd, i: (bi, 0, hd, 0))],
        out_specs=pl.BlockSpec((1, tq, MLA_V), lambda bi, hd, i: (bi, i, hd)),
        out_shape=jax.ShapeDtypeStruct((b, s, MLA_HEADS * MLA_V), BF16),
        scratch_shapes=[pltpu.VMEM((tk + c, tq), F32), pltpu.VMEM((tk + c, tq), F32),
                        pltpu.VMEM((MLA_V, tq), F32), pltpu.VMEM((1, tq), F32)],
        compiler_params=_params("arbitrary", "arbitrary", "arbitrary"),
        name="mla_attn",
    )(qt, qt, k, vt, kc, vtc)


def _swa_attn_kernel(sink_ref, q_ref, k_ref, v_ref, kc_ref, vc_ref, o_ref, *, tq):
    kvh = pl.program_id(1)
    i = pl.program_id(2)
    s_len = k_ref.shape[1]
    blk = LANE
    span = blk + 2 * WINDOW
    rows = SWA_GROUP * blk
    kc = kc_ref[0]
    vc = vc_ref[0]
    low_half = lax.broadcasted_iota(jnp.int32, (blk, LANE), 1) < SWA_DIM
    row_id = lax.broadcasted_iota(jnp.int32, (rows, 1), 0)
    head_of_row = row_id // blk
    sink = jnp.zeros((rows, 1), F32)
    for g in range(SWA_GROUP):
        sink = jnp.where(head_of_row == g, sink_ref[kvh * SWA_GROUP + g] * LOG2E, sink)
    for r in range(tq // blk):
        q0 = i * tq + r * blk
        start = pl.multiple_of(jnp.clip(q0 - WINDOW, 0, s_len - span), LANE)
        kw = k_ref[0, pl.ds(start, span), :]
        vw = v_ref[0, pl.ds(start, span), :]
        qpos = q0 + lax.broadcasted_iota(jnp.int32, (rows, span), 0) % blk
        kpos = start + lax.broadcasted_iota(jnp.int32, (rows, span), 1)
        band = jnp.abs(qpos - kpos) <= WINDOW
        q_rows = []
        for g in range(SWA_GROUP):
            q_t = q_ref[0, r * blk:(r + 1) * blk, (g // 2) * LANE:(g // 2 + 1) * LANE]
            keep = low_half if g % 2 == 0 else jnp.logical_not(low_half)
            q_rows.append(jnp.where(keep, q_t, jnp.zeros_like(q_t)))
        q4 = jnp.concatenate(q_rows, axis=0)
        s_w = jnp.where(band, _dot_nt(q4, kw), NEG)
        s_c = _dot_nt(q4, kc)
        m = jnp.maximum(jnp.maximum(jnp.max(s_w, axis=1, keepdims=True),
                                    jnp.max(s_c, axis=1, keepdims=True)), sink)
        p_w = jnp.exp2(s_w - m)
        p_c = jnp.exp2(s_c - m)
        l = jnp.sum(p_w, axis=1, keepdims=True) + jnp.sum(p_c, axis=1, keepdims=True) + jnp.exp2(sink - m)
        o = (_dot(p_w.astype(vw.dtype), vw) + _dot(p_c.astype(vc.dtype), vc)) / l
        for t in range(SWA_GROUP // 2):
            pair = jnp.where(low_half, o[2 * t * blk:(2 * t + 1) * blk], o[(2 * t + 1) * blk:(2 * t + 2) * blk])
            o_ref[0, r * blk:(r + 1) * blk, t * LANE:(t + 1) * LANE] = pair.astype(o_ref.dtype)


def _swa_attn(sink, q, k, v, kc, vc, tq):
    b, s, _ = q.shape
    c = kc.shape[1]
    gw = SWA_GROUP * SWA_DIM
    return pl.pallas_call(
        functools.partial(_swa_attn_kernel, tq=tq),
        grid=(b, SWA_KV_HEADS, s // tq),
        in_specs=[pl.BlockSpec(memory_space=pltpu.SMEM),
                  pl.BlockSpec((1, tq, gw), lambda bi, kh, i: (bi, i, kh)),
                  pl.BlockSpec((1, s, LANE), lambda bi, kh, i: (bi, 0, kh)),
                  pl.BlockSpec((1, s, LANE), lambda bi, kh, i: (bi, 0, kh)),
                  pl.BlockSpec((1, c, LANE), lambda bi, kh, i: (bi, 0, kh)),
                  pl.BlockSpec((1, c, LANE), lambda bi, kh, i: (bi, 0, kh))],
        out_specs=pl.BlockSpec((1, tq, gw), lambda bi, kh, i: (bi, i, kh)),
        out_shape=jax.ShapeDtypeStruct((b, s, SWA_HEADS * SWA_DIM), BF16),
        compiler_params=_params("arbitrary", "arbitrary", "arbitrary"),
        name="swa_attn",
    )(sink, q, k, v, kc, vc)


def _merge_kernel(a_ref, b_ref, ga_ref, gb_ref, mga_ref, mgb_ref, x_ref, gate_ref,
                  wba_ref, wbb_ref, wo_ref, fg_ref, o_ref):
    a_in = (a_ref[0].astype(F32) * ga_ref[0].astype(F32)).astype(BF16)
    b_in = (b_ref[0].astype(F32) * gb_ref[0].astype(F32)).astype(BF16)
    y_a = _dot(a_in, wba_ref[...])
    y_b = _dot(b_in, wbb_ref[...])
    mix = mga_ref[0].astype(F32) * y_a + mgb_ref[0].astype(F32) * y_b
    out = _dot(mix.astype(BF16), wo_ref[...])
    xn = x_ref[0] + gate_ref[0] * out
    o_ref[0] = _rms(xn, fg_ref[...]).astype(o_ref.dtype)


def _merge(a, bh, gates, x, mod, wba, wbb, wo, fg, tm):
    b, s, d = x.shape
    wa = a.shape[2]
    const = lambda bi, i: (0, 0)
    row = lambda col: (lambda bi, i: (bi, i, col))
    return pl.pallas_call(
        _merge_kernel,
        grid=(b, s // tm),
        in_specs=[pl.BlockSpec((1, tm, wa), row(0)),
                  pl.BlockSpec((1, tm, wa), row(0)),
                  pl.BlockSpec((1, tm, wa), row(0)),
                  pl.BlockSpec((1, tm, wa), row(1)),
                  pl.BlockSpec((1, tm, d), row(1)),
                  pl.BlockSpec((1, tm, d), row(2)),
                  pl.BlockSpec((1, tm, d), row(0)),
                  pl.BlockSpec((1, 1, d), lambda bi, i: (bi, 0, 2)),
                  pl.BlockSpec((wa, d), const, pipeline_mode=pl.Buffered(1)),
                  pl.BlockSpec((wa, d), const, pipeline_mode=pl.Buffered(1)),
                  pl.BlockSpec((d, d), const, pipeline_mode=pl.Buffered(1)),
                  pl.BlockSpec((1, d), const)],
        out_specs=pl.BlockSpec((1, tm, d), row(0)),
        out_shape=jax.ShapeDtypeStruct((b, s, d), x.dtype),
        compiler_params=_params("arbitrary", "arbitrary"),
        name="merge_out",
    )(a, bh, gates, gates, gates, gates, x, mod, wba, wbb, wo, fg.reshape(1, d))


def _rope_tables(s):
    t = jnp.arange(s, dtype=jnp.int32)
    n = 16
    inv = ROPE_BASE ** (-jnp.arange(n, dtype=F32) / n)
    ang_r = (t // GRID_W).astype(F32)[:, None] * inv[None, :]
    ang_c = (t % GRID_W).astype(F32)[:, None] * inv[None, :]
    cos = jnp.concatenate([jnp.cos(ang_r)] * 2 + [jnp.cos(ang_c)] * 2, axis=1)
    sin = jnp.concatenate([-jnp.sin(ang_r), jnp.sin(ang_r), -jnp.sin(ang_c), jnp.sin(ang_c)], axis=1)
    return jnp.tile(cos, (1, 2)), jnp.tile(sin, (1, 2))


def _dup_heads(w, heads, dim):
    lead = w.shape[:-1]
    w = w.reshape(lead + (heads, 1, dim))
    return jnp.broadcast_to(w, lead + (heads, 2, dim)).reshape(lead + (heads * 2 * dim,))


def kernel(x, c, ctx, c_ctx, w_ada, b_ada, norm_g, w_in, b_in, q_norm_g, kv_norm_g, w_uq, w_ukv, sink,
           w_branch_a, w_branch_b, w_out, final_g):
    bsz, s, d = x.shape
    c_len = ctx.shape[1]
    assert w_ada.shape[0] == 1, "single-layer block"
    widths = (MLA_RANK, MLA_RANK, MLA_ROPE, MLA_HEADS * MLA_V, SWA_HEADS * SWA_DIM,
              SWA_KV_HEADS * SWA_DIM, SWA_KV_HEADS * SWA_DIM, SWA_HEADS * SWA_DIM, d, d)
    assert sum(widths) == w_in.shape[2]
    offs = [0]
    for w_ in widths:
        offs.append(offs[-1] + w_)
    w_in0, b_in0 = w_in[0], b_in[0]
    col = lambda k: (w_in0[:, offs[k]:offs[k + 1]], b_in0[offs[k]:offs[k + 1]])
    (w_cq, b_cq), (w_ckv, b_ckv), (w_kpe, b_kpe), (w_ga, b_ga), (w_qs, b_qs) = (col(k) for k in range(5))
    (w_ks, b_ks), (w_vs, b_vs), (w_gb, b_gb), (w_mga, b_mga), (w_mgb, b_mgb) = (col(k) for k in range(5, 10))

    rows = 8
    c_rows = jnp.concatenate([c, c_ctx[None, :], jnp.zeros((rows - bsz - 1, d), F32)], axis=0)
    mod = _modulation(c_rows, w_ada, b_ada[0])
    mod_lat = mod[:bsz].reshape(bsz, 1, 3 * d)
    mod_ctx = jnp.broadcast_to(mod[bsz].reshape(1, 1, 3 * d), (bsz, 1, 3 * d))

    h = _norm_mod(x, norm_g[0], mod_lat, NORM_TM)
    h_c = _norm_mod(ctx, norm_g[0], mod_ctx, c_len)

    pad_pe = LANE - MLA_ROPE
    w_mla = jnp.concatenate([w_cq, w_ckv, w_kpe, jnp.zeros((d, pad_pe), F32)], axis=1).astype(BF16)
    b_mla = jnp.concatenate([b_cq, b_ckv, b_kpe, jnp.zeros((pad_pe,), F32)])
    wuq = jnp.pad(w_uq[0].reshape(MLA_RANK, MLA_HEADS, MLA_QK),
                  ((0, 0), (0, 0), (0, MLA_QK_PAD - MLA_QK))).reshape(MLA_RANK, -1).astype(BF16)
    wukv = w_ukv[0].astype(BF16)
    w_swa = jnp.concatenate([w_qs, _dup_heads(w_ks, SWA_KV_HEADS, SWA_DIM),
                             _dup_heads(w_vs, SWA_KV_HEADS, SWA_DIM)], axis=1).astype(BF16)
    b_swa = jnp.concatenate([b_qs, _dup_heads(b_ks, SWA_KV_HEADS, SWA_DIM),
                             _dup_heads(b_vs, SWA_KV_HEADS, SWA_DIM)])
    w_gates = jnp.concatenate([w_ga, w_gb, w_mga, w_mgb], axis=1).astype(BF16)
    b_gates = jnp.concatenate([b_ga, b_gb, b_mga, b_mgb])

    cos, sin = _rope_tables(s)
    cos_c = jnp.ones((c_len, LANE), F32)
    sin_c = jnp.zeros((c_len, LANE), F32)

    q_mla, k_mla, vt_mla = _mla_proj(h, w_mla, b_mla, q_norm_g[0], kv_norm_g[0], wuq, wukv, cos, sin,
                                     PROJ_TM, MLA_TK)
    _, kc_mla, vtc_mla = _mla_proj(h_c, w_mla, b_mla, q_norm_g[0], kv_norm_g[0], wuq, wukv, cos_c, sin_c,
                                   c_len, c_len)
    q_s, k_s, v_s = _swa_proj(h, w_swa, b_swa, cos, sin, PROJ_TM)
    _, kc_s, vc_s = _swa_proj(h_c, w_swa, b_swa, cos_c, sin_c, c_len)
    gates = _gates(h, w_gates, b_gates, 2 * MLA_HEADS * MLA_V, GATE_TM, GATE_TN)

    a_heads = _mla_attn(q_mla, k_mla, vt_mla, kc_mla, vtc_mla, MLA_TQ, MLA_SUB)
    b_heads = _swa_attn(sink[0], q_s, k_s, v_s, kc_s, vc_s, SWA_TQ)

    return _merge(a_heads, b_heads, gates, x, mod_lat, w_branch_a[0].astype(BF16),
                  w_branch_b[0].astype(BF16), w_out[0].astype(BF16), final_g, MERGE_TM)
```
